```python
import jax, jax.numpy as jnp
from jax import lax
import numpy as np

D_MODEL = 1024
BATCH = 8
SEQ = 8192
DEPTH = 4
DEC_BATCH = 8
DEC_SEQ = 4096
PAST_LEN = 128

N_MIXERS = 2
N_ATTN_LAYERS = (DEPTH + 1) // 2
N_REC_LAYERS = DEPTH // 2
HEAD_DIM = 64
N_Q_HEADS = D_MODEL // HEAD_DIM
N_KV_HEADS = N_Q_HEADS // 4
GQA_GROUP = N_Q_HEADS // N_KV_HEADS
WINDOW = 128
ATTN_BLOCK = 128
ATTN_IN_DIM = (N_Q_HEADS + 2 * N_KV_HEADS) * HEAD_DIM
REC_EXPAND = 128
REC_HEADS = D_MODEL // REC_EXPAND
REC_F_DIM = REC_HEADS * REC_EXPAND
REC_V_DIM = D_MODEL // REC_HEADS
REC_IN_DIM = 3 * REC_F_DIM + 2 * D_MODEL
REC_CHUNK = 64
D_FF = 2816
EPS = 1e-6

kernel_name = "hybrid_bidir_swa_hgrn2_macaron"


def rmsnorm(x, g):
    xf = x.astype(jnp.float32)
    y = xf * lax.rsqrt(jnp.mean(xf * xf, axis=-1, keepdims=True) + EPS)
    return (y * g.astype(jnp.float32)).astype(x.dtype)


def swiglu(x, w_gate, w_up, w_down):
    return (jax.nn.silu(x @ w_gate) * (x @ w_up)) @ w_down


def alibi_slopes():
    return jnp.asarray(2.0 ** (-8.0 * np.arange(1, N_Q_HEADS + 1) / N_Q_HEADS), dtype=jnp.float32)


def windowed_attention(h, w_in, q_gain, k_gain, sink, w_out):
    B, S, _ = h.shape
    nb = S // ATTN_BLOCK
    n_side = WINDOW // ATTN_BLOCK
    kb_len = (2 * n_side + 1) * ATTN_BLOCK
    qkv = (h @ w_in).astype(jnp.float32)
    q, k, v = jnp.split(qkv, [N_Q_HEADS * HEAD_DIM, (N_Q_HEADS + N_KV_HEADS) * HEAD_DIM], axis=-1)
    q = rmsnorm(q.reshape(B, S, N_Q_HEADS, HEAD_DIM), q_gain)
    k = rmsnorm(k.reshape(B, S, N_KV_HEADS, HEAD_DIM), k_gain)
    v = v.reshape(B, S, N_KV_HEADS, HEAD_DIM)
    q = q.reshape(B, nb, ATTN_BLOCK, N_KV_HEADS, GQA_GROUP, HEAD_DIM)
    pad = ((0, 0), (WINDOW, WINDOW), (0, 0), (0, 0))
    kp = jnp.pad(k, pad).reshape(B, nb + 2 * n_side, ATTN_BLOCK, N_KV_HEADS, HEAD_DIM)
    vp = jnp.pad(v, pad).reshape(B, nb + 2 * n_side, ATTN_BLOCK, N_KV_HEADS, HEAD_DIM)
    kb = jnp.concatenate([kp[:, j:j + nb] for j in range(2 * n_side + 1)], axis=2)
    vb = jnp.concatenate([vp[:, j:j + nb] for j in range(2 * n_side + 1)], axis=2)
    scores = jnp.einsum('bnqhgd,bnkhd->bnhgqk', q, kb) * (HEAD_DIM ** -0.5)
    a = jnp.arange(ATTN_BLOCK)[:, None]
    c = jnp.arange(kb_len)[None, :]
    dist = c - WINDOW - a
    kpos = jnp.arange(nb)[:, None] * ATTN_BLOCK + jnp.arange(kb_len)[None, :] - WINDOW
    mask = ((kpos >= 0) & (kpos < S))[:, None, :] & (jnp.abs(dist) <= WINDOW)[None]
    slopes = alibi_slopes().reshape(N_KV_HEADS, GQA_GROUP)
    bias = -slopes[:, :, None, None] * jnp.abs(dist).astype(jnp.float32)[None, None]
    scores = jnp.where(mask[None, :, None, None], scores + bias[None, None], -jnp.inf)
    sink_l = sink.astype(jnp.float32).reshape(N_KV_HEADS, GQA_GROUP)[None, None, :, :, None, None]
    m = jnp.maximum(jnp.max(scores, axis=-1, keepdims=True), sink_l)
    p = jnp.exp(scores - m)
    denom = jnp.sum(p, axis=-1, keepdims=True) + jnp.exp(sink_l - m)
    out = jnp.einsum('bnhgqk,bnkhd->bnqhgd', p / denom, vb)
    return out.reshape(B, S, D_MODEL).astype(h.dtype) @ w_out


def hgrn2_chunk_scan(q, k, v, log_f):
    N, S, H, dk = q.shape
    dv = v.shape[-1]
    nc = S // REC_CHUNK

    def to_chunks(t):
        return t.reshape(N, nc, REC_CHUNK, H, t.shape[-1]).transpose(1, 0, 3, 2, 4)

    causal = jnp.tril(jnp.ones((REC_CHUNK, REC_CHUNK), dtype=bool))

    def step(state, xs):
        qc, kc, vc, lfc = xs
        b = jnp.cumsum(lfc, axis=2)
        o_inter = jnp.einsum('nhck,nhkv->nhcv', qc * jnp.exp(b), state)
        diff = b[:, :, :, None, :] - b[:, :, None, :, :]
        decay = jnp.exp(jnp.where(causal[None, None, :, :, None], diff, -jnp.inf))
        attn = jnp.einsum('nhtk,nhsk,nhtsk->nhts', qc, kc, decay)
        o_intra = jnp.einsum('nhts,nhsv->nhtv', attn, vc)
        b_last = b[:, :, -1:, :]
        k_dec = kc * jnp.exp(b_last - b)
        new_state = jnp.exp(b_last[:, :, 0, :])[..., None] * state + jnp.einsum('nhsk,nhsv->nhkv', k_dec, vc)
        return new_state, o_inter + o_intra

    state0 = jnp.zeros((N, H, dk, dv), jnp.float32)
    _, o = lax.scan(step, state0, (to_chunks(q), to_chunks(k), to_chunks(v), to_chunks(log_f)))
    return o.transpose(1, 0, 3, 2, 4).reshape(N, S, H, dv)


def hgrn2_bidirectional(h, w_in, lb, o_gain, w_out):
    B, S, _ = h.shape
    proj = (h @ w_in).astype(jnp.float32)
    q, z_fw, z_bw, i, g = jnp.split(
        proj, [REC_F_DIM, 2 * REC_F_DIM, 3 * REC_F_DIM, 3 * REC_F_DIM + D_MODEL], axis=-1)
    z = jnp.stack([z_fw, jnp.flip(z_bw, axis=1)], axis=0)
    lbf = lb.astype(jnp.float32)[:, None, None, :]
    log_f = jnp.logaddexp(jnp.log(lbf), jnp.log1p(-lbf) + jax.nn.log_sigmoid(z))
    k = (1.0 - lbf) * jax.nn.sigmoid(-z)
    q2 = jnp.stack([q, jnp.flip(q, axis=1)], axis=0)
    i2 = jnp.stack([i, jnp.flip(i, axis=1)], axis=0)
    hd = lambda t, d: t.reshape(2 * B, S, REC_HEADS, d)
    o = hgrn2_chunk_scan(hd(q2, REC_EXPAND), hd(k, REC_EXPAND), hd(i2, REC_V_DIM), hd(log_f, REC_EXPAND))
    o = o.reshape(2, B, S, REC_HEADS, REC_V_DIM)
    o = o[0] + jnp.flip(o[1], axis=1)
    o = rmsnorm(o, o_gain) * jax.nn.silu(g.reshape(B, S, REC_HEADS, REC_V_DIM))
    return o.reshape(B, S, D_MODEL).astype(h.dtype) @ w_out


def run_trunk(x, ffn1_norm, ffn1_w_gate, ffn1_w_up, ffn1_w_down, mix_norm,
              attn_w_in, attn_q_gain, attn_k_gain, attn_sink, attn_w_out,
              rec_w_in, lb_all, rec_o_gain, rec_w_out,
              ffn2_norm, ffn2_w_gate, ffn2_w_up, ffn2_w_down, out_norm):
    for layer in range(DEPTH):
        h = x + 0.5 * swiglu(rmsnorm(x, ffn1_norm[layer]), ffn1_w_gate[layer], ffn1_w_up[layer], ffn1_w_down[layer])
        hn = rmsnorm(h, mix_norm[layer])
        j = layer // N_MIXERS
        if layer % N_MIXERS == 0:
            h = h + windowed_attention(hn, attn_w_in[j], attn_q_gain[j], attn_k_gain[j], attn_sink[j], attn_w_out[j])
        else:
            h = h + hgrn2_bidirectional(hn, rec_w_in[j], lb_all[j], rec_o_gain[j], rec_w_out[j])
        h = h + 0.5 * swiglu(rmsnorm(h, ffn2_norm[layer]), ffn2_w_gate[layer], ffn2_w_up[layer], ffn2_w_down[layer])
        x = rmsnorm(h, out_norm[layer])
    return x


def setup_inputs(seed: int = 0) -> dict:
    key = jax.random.key(seed)
    ks = jax.random.split(key, 32)
    f32 = jnp.float32
    nrm = lambda k, shape, fan_in: jax.random.normal(k, shape, f32) * (fan_in ** -0.5)
    gain = lambda k, shape: 1.0 + 0.05 * jax.random.normal(k, shape, f32)
    return {
        "x_prompt": jax.random.normal(ks[0], (BATCH, SEQ, D_MODEL), f32),
        "x_sample": jax.random.normal(ks[1], (DEC_BATCH, DEC_SEQ, D_MODEL), f32),
        "ffn1_norm": gain(ks[2], (DEPTH, D_MODEL)),
        "ffn1_w_gate": nrm(ks[3], (DEPTH, D_MODEL, D_FF), D_MODEL),
        "ffn1_w_up": nrm(ks[4], (DEPTH, D_MODEL, D_FF), D_MODEL),
        "ffn1_w_down": nrm(ks[5], (DEPTH, D_FF, D_MODEL), D_FF),
        "mix_norm": gain(ks[6], (DEPTH, D_MODEL)),
        "attn_w_in": nrm(ks[7], (N_ATTN_LAYERS, D_MODEL, ATTN_IN_DIM), D_MODEL),
        "attn_q_gain": gain(ks[8], (N_ATTN_LAYERS, HEAD_DIM)),
        "attn_k_gain": gain(ks[9], (N_ATTN_LAYERS, HEAD_DIM)),
        "attn_sink": jax.random.normal(ks[10], (N_ATTN_LAYERS, N_Q_HEADS), f32),
        "attn_w_out": nrm(ks[11], (N_ATTN_LAYERS, D_MODEL, D_MODEL), D_MODEL),
        "rec_w_in": nrm(ks[12], (N_REC_LAYERS, D_MODEL, REC_IN_DIM), D_MODEL),
        "rec_lb_logits": jax.random.normal(ks[13], (N_REC_LAYERS, 2, REC_F_DIM), f32),
        "rec_o_gain": gain(ks[14], (N_REC_LAYERS, REC_V_DIM)),
        "rec_w_out": nrm(ks[15], (N_REC_LAYERS, D_MODEL, D_MODEL), D_MODEL),
        "ffn2_norm": gain(ks[16], (DEPTH, D_MODEL)),
        "ffn2_w_gate": nrm(ks[17], (DEPTH, D_MODEL, D_FF), D_MODEL),
        "ffn2_w_up": nrm(ks[18], (DEPTH, D_MODEL, D_FF), D_MODEL),
        "ffn2_w_down": nrm(ks[19], (DEPTH, D_FF, D_MODEL), D_FF),
        "out_norm": gain(ks[20], (DEPTH, D_MODEL)),
    }


def reference(x_prompt, x_sample, ffn1_norm, ffn1_w_gate, ffn1_w_up, ffn1_w_down, mix_norm,
              attn_w_in, attn_q_gain, attn_k_gain, attn_sink, attn_w_out,
              rec_w_in, rec_lb_logits, rec_o_gain, rec_w_out,
              ffn2_norm, ffn2_w_gate, ffn2_w_up, ffn2_w_down, out_norm):
    lb_all = jnp.cumsum(jax.nn.softmax(rec_lb_logits.astype(jnp.float32), axis=0), axis=0)
    lb_all = lb_all - lb_all[0:1]
    y_prompt = run_trunk(x_prompt, ffn1_norm, ffn1_w_gate, ffn1_w_up, ffn1_w_down, mix_norm,
                         attn_w_in, attn_q_gain, attn_k_gain, attn_sink, attn_w_out,
                         rec_w_in, lb_all, rec_o_gain, rec_w_out,
                         ffn2_norm, ffn2_w_gate, ffn2_w_up, ffn2_w_down, out_norm)
    y_sample = run_trunk(x_sample, ffn1_norm, ffn1_w_gate, ffn1_w_up, ffn1_w_down, mix_norm,
                         attn_w_in, attn_q_gain, attn_k_gain, attn_sink, attn_w_out,
                         rec_w_in, lb_all, rec_o_gain, rec_w_out,
                         ffn2_norm, ffn2_w_gate, ffn2_w_up, ffn2_w_down, out_norm)
    return (y_prompt, y_sample)
```

```python
import functools

import numpy as np
import jax
import jax.numpy as jnp
from jax import lax
from jax.experimental import pallas as pl
from jax.experimental.pallas import tpu as pltpu

D_MODEL = 1024
DEPTH = 4
N_MIXERS = 2
HEAD_DIM = 64
N_Q_HEADS = D_MODEL // HEAD_DIM
N_KV_HEADS = N_Q_HEADS // 4
GQA_GROUP = N_Q_HEADS // N_KV_HEADS
KV_DIM = N_KV_HEADS * HEAD_DIM
WINDOW = 128
ATTN_BLOCK = 128
REC_EXPAND = 128
REC_HEADS = D_MODEL // REC_EXPAND
REC_V_DIM = D_MODEL // REC_HEADS
D_FF = 2816
EPS = 1e-6

F32 = jnp.float32
BF16 = jnp.bfloat16

FFN_TM = 512
FFN_CHUNK = 256
QKV_TM = 512
REC_PROJ_TM = 256
REC_OUT_TM = 512
SCAN_CHUNK = 64
SCAN_BLOCK = 256
VMEM_LIMIT = 56 * 1024 * 1024

_NT = (((1,), (1,)), ((), ()))
_TN = (((0,), (0,)), ((), ()))


def _const_spec(shape):
    return pl.BlockSpec(shape, lambda *_: (0,) * len(shape))


def _rms_scale(x):
    return lax.rsqrt(jnp.mean(x * x, axis=-1, keepdims=True) + EPS)


def _ffn_kernel(x_ref, g1_ref, wg_ref, wu_ref, wd_ref, g2_ref, *rest, final):
    if final:
        y_ref, acc_ref = rest
    else:
        h_ref, hn_ref, acc_ref = rest
    x = x_ref[...]
    n = (x * _rms_scale(x) * g1_ref[...]).astype(BF16)
    for c in range(D_FF // FFN_CHUNK):
        sl = slice(c * FFN_CHUNK, (c + 1) * FFN_CHUNK)
        g = jnp.dot(n, wg_ref[:, sl], preferred_element_type=F32)
        u = jnp.dot(n, wu_ref[:, sl], preferred_element_type=F32)
        a = (g * jax.nn.sigmoid(g) * u).astype(BF16)
        d = jnp.dot(a, wd_ref[sl, :], preferred_element_type=F32)
        if c == 0:
            acc_ref[...] = d
        else:
            acc_ref[...] += d
    h = x + 0.5 * acc_ref[...]
    hn = h * _rms_scale(h) * g2_ref[...]
    if final:
        y_ref[...] = hn
    else:
        h_ref[...] = h
        hn_ref[...] = hn.astype(BF16)


def _ffn(x, g1, wg, wu, wd, g2, *, final):
    t = x.shape[0]
    tm = min(FFN_TM, t)
    row = pl.BlockSpec((tm, D_MODEL), lambda i: (i, 0))
    if final:
        out_shape = jax.ShapeDtypeStruct((t, D_MODEL), F32)
        out_specs = row
    else:
        out_shape = (jax.ShapeDtypeStruct((t, D_MODEL), F32), jax.ShapeDtypeStruct((t, D_MODEL), BF16))
        out_specs = (row, row)
    return pl.pallas_call(
        functools.partial(_ffn_kernel, final=final),
        grid=(t // tm,),
        in_specs=[row, _const_spec((1, D_MODEL)), _const_spec((D_MODEL, D_FF)), _const_spec((D_MODEL, D_FF)),
                  _const_spec((D_FF, D_MODEL)), _const_spec((1, D_MODEL))],
        out_specs=out_specs,
        out_shape=out_shape,
        scratch_shapes=[pltpu.VMEM((tm, D_MODEL), F32)],
        compiler_params=pltpu.CompilerParams(dimension_semantics=("parallel",), vmem_limit_bytes=VMEM_LIMIT),
        name="ffn_final" if final else "ffn_mid",
    )(x, g1, wg, wu, wd, g2)


def _group_mean_matrix(n, group):
    idx = np.arange(n) // group
    return jnp.asarray((idx[:, None] == idx[None, :]).astype(np.float32) / group, dtype=BF16)


def _qkv_kernel(hn_ref, w_ref, gq_ref, gk_ref, gm_ref, q_ref, k_ref, v_ref):
    hn = hn_ref[...]
    gm = gm_ref[...]
    qd = N_Q_HEADS * HEAD_DIM
    for c in range(qd // KV_DIM):
        sl = slice(c * KV_DIM, (c + 1) * KV_DIM)
        q = jnp.dot(hn, w_ref[:, sl], preferred_element_type=F32)
        ms = jnp.dot((q * q).astype(BF16), gm, preferred_element_type=F32)
        q_ref[:, sl] = (q * lax.rsqrt(ms + EPS) * gq_ref[:, sl]).astype(BF16)
    k = jnp.dot(hn, w_ref[:, qd:qd + KV_DIM], preferred_element_type=F32)
    ms = jnp.dot((k * k).astype(BF16), gm, preferred_element_type=F32)
    k_ref[...] = (k * lax.rsqrt(ms + EPS) * gk_ref[...]).astype(BF16)
    v_ref[...] = jnp.dot(hn, w_ref[:, qd + KV_DIM:], preferred_element_type=F32).astype(BF16)


def _attn_qkv(hn, w_in, gq, gk):
    t = hn.shape[0]
    tm = min(QKV_TM, t)
    in_dim = w_in.shape[1]
    return pl.pallas_call(
        _qkv_kernel,
        grid=(t // tm,),
        in_specs=[pl.BlockSpec((tm, D_MODEL), lambda i: (i, 0)), _const_spec((D_MODEL, in_dim)),
                  _const_spec((1, D_MODEL)), _const_spec((1, KV_DIM)), _const_spec((KV_DIM, KV_DIM))],
        out_specs=(pl.BlockSpec((tm, D_MODEL), lambda i: (i, 0)), pl.BlockSpec((tm, KV_DIM), lambda i: (i, 0)),
                   pl.BlockSpec((tm, KV_DIM), lambda i: (i, 0))),
        out_shape=(jax.ShapeDtypeStruct((t, D_MODEL), BF16), jax.ShapeDtypeStruct((t, KV_DIM), BF16),
                   jax.ShapeDtypeStruct((t, KV_DIM), BF16)),
        compiler_params=pltpu.CompilerParams(dimension_semantics=("parallel",), vmem_limit_bytes=VMEM_LIMIT),
        name="attn_qkv",
    )(hn, w_in, gq, gk, _group_mean_matrix(KV_DIM, HEAD_DIM))


_ALIBI_SLOPES = [float(2.0 ** (-8.0 * (i + 1) / N_Q_HEADS)) for i in range(N_Q_HEADS)]


def _attn_core_kernel(sink_ref, q_ref, kp_ref, kc_ref, kn_ref, vp_ref, vc_ref, vn_ref, h_ref, wo_ref,
                      out_ref, att_ref, *, seq_len):
    n = pl.program_id(1)
    kb = 3 * ATTN_BLOCK
    row = lax.broadcasted_iota(jnp.int32, (ATTN_BLOCK, kb), 0)
    col = lax.broadcasted_iota(jnp.int32, (ATTN_BLOCK, kb), 1)
    dist = col - WINDOW - row
    absd = jnp.abs(dist)
    kpos = col + (n - 1) * ATTN_BLOCK
    mask = (absd <= WINDOW) & (kpos >= 0) & (kpos < seq_len)
    absd_f = absd.astype(F32)
    for g in range(N_KV_HEADS):
        ks = slice(g * HEAD_DIM, (g + 1) * HEAD_DIM)
        kg = jnp.concatenate([kp_ref[0, :, ks], kc_ref[0, :, ks], kn_ref[0, :, ks]], axis=0)
        vg = jnp.concatenate([vp_ref[0, :, ks], vc_ref[0, :, ks], vn_ref[0, :, ks]], axis=0)
        for j in range(GQA_GROUP):
            hd = g * GQA_GROUP + j
            qs = slice(hd * HEAD_DIM, (hd + 1) * HEAD_DIM)
            s = lax.dot_general(q_ref[0, :, qs], kg, _NT, preferred_element_type=F32)
            s = jnp.where(mask, s - _ALIBI_SLOPES[hd] * absd_f, -jnp.inf)
            sink = sink_ref[hd]
            m = jnp.maximum(jnp.max(s, axis=-1, keepdims=True), sink)
            p = jnp.exp(s - m)
            denom = jnp.sum(p, axis=-1, keepdims=True) + jnp.exp(sink - m)
            o = jnp.dot(p.astype(BF16), vg, preferred_element_type=F32) / denom
            att_ref[:, qs] = o.astype(BF16)
    out_ref[0] = h_ref[0] + jnp.dot(att_ref[...], wo_ref[...], preferred_element_type=F32)


def _attn_core(q, k, v, sink, h, w_out):
    b, s, _ = q.shape
    nb = s // ATTN_BLOCK
    cur = lambda bi, n: (bi, n, 0)
    prev = lambda bi, n: (bi, jnp.maximum(n - 1, 0), 0)
    nxt = lambda bi, n: (bi, jnp.minimum(n + 1, nb - 1), 0)
    kv_blk = (1, ATTN_BLOCK, KV_DIM)
    d_blk = (1, ATTN_BLOCK, D_MODEL)
    return pl.pallas_call(
        functools.partial(_attn_core_kernel, seq_len=s),
        grid=(b, nb),
        in_specs=[pl.BlockSpec(memory_space=pltpu.SMEM),
                  pl.BlockSpec(d_blk, cur),
                  pl.BlockSpec(kv_blk, prev), pl.BlockSpec(kv_blk, cur), pl.BlockSpec(kv_blk, nxt),
                  pl.BlockSpec(kv_blk, prev), pl.BlockSpec(kv_blk, cur), pl.BlockSpec(kv_blk, nxt),
                  pl.BlockSpec(d_blk, cur), _const_spec((D_MODEL, D_MODEL))],
        out_specs=pl.BlockSpec(d_blk, cur),
        out_shape=jax.ShapeDtypeStruct((b, s, D_MODEL), F32),
        scratch_shapes=[pltpu.VMEM((ATTN_BLOCK, D_MODEL), BF16)],
        compiler_params=pltpu.CompilerParams(dimension_semantics=("parallel", "parallel"),
                                             vmem_limit_bytes=VMEM_LIMIT),
        name="attn_core",
    )(sink, q, k, k, k, v, v, v, h, w_out)


def _rec_proj_kernel(hn_ref, w_ref, lb_ref, q_ref, lf_fw_ref, lf_bw_ref, k_fw_ref, k_bw_ref, i_ref, g_ref):
    hn = hn_ref[...]
    f = D_MODEL
    q_ref[...] = jnp.dot(hn, w_ref[:, 0:f], preferred_element_type=F32)
    for d, (lf_ref, k_ref) in enumerate(((lf_fw_ref, k_fw_ref), (lf_bw_ref, k_bw_ref))):
        z = jnp.dot(hn, w_ref[:, (1 + d) * f:(2 + d) * f], preferred_element_type=F32)
        lb = lb_ref[d:d + 1, :]
        e = jnp.exp(-jnp.abs(z))
        log_sig = jnp.minimum(z, 0.0) - jnp.log1p(e)
        lf_ref[...] = jnp.logaddexp(jnp.log(lb), jnp.log1p(-lb) + log_sig)
        k_ref[...] = (1.0 - lb) * (jnp.where(z >= 0.0, e, 1.0) / (1.0 + e))
    i_ref[...] = jnp.dot(hn, w_ref[:, 3 * f:4 * f], preferred_element_type=F32).astype(BF16)
    g_ref[...] = jnp.dot(hn, w_ref[:, 4 * f:5 * f], preferred_element_type=F32).astype(BF16)


def _rec_proj(hn, w_in, lb):
    t = hn.shape[0]
    tm = min(REC_PROJ_TM, t)
    row = pl.BlockSpec((tm, D_MODEL), lambda i: (i, 0))
    f32o = jax.ShapeDtypeStruct((t, D_MODEL), F32)
    bf16o = jax.ShapeDtypeStruct((t, D_MODEL), BF16)
    return pl.pallas_call(
        _rec_proj_kernel,
        grid=(t // tm,),
        in_specs=[row, _const_spec((D_MODEL, w_in.shape[1])), _const_spec((2, D_MODEL))],
        out_specs=(row,) * 7,
        out_shape=(f32o, f32o, f32o, f32o, f32o, bf16o, bf16o),
        compiler_params=pltpu.CompilerParams(dimension_semantics=("parallel",), vmem_limit_bytes=VMEM_LIMIT),
        name="rec_proj",
    )(hn, w_in, lb)


def _scan_chunk(q, k, lf, v, st, *, reverse):
    c = q.shape[0]
    ridx = lax.broadcasted_iota(jnp.int32, q.shape, 0)
    rowi = lax.broadcasted_iota(jnp.int32, (c, c), 0)
    coli = lax.broadcasted_iota(jnp.int32, (c, c), 1)
    tot = lf
    qsum = lf
    ksum = jnp.zeros_like(lf)
    att = jnp.zeros((c, c), F32)
    m = 1
    while m < c:
        odd = (ridx & m) != 0
        q_rows = ~odd if reverse else odd
        z = jnp.where(q_rows, qsum, ksum)
        x = (jnp.where(q_rows, q, k) * jnp.exp(z)).astype(BF16)
        gram = lax.dot_general(x, x, _NT, preferred_element_type=F32)
        r_odd = (rowi & m) != 0
        c_odd = (coli & m) != 0
        same = (rowi & ~(2 * m - 1)) == (coli & ~(2 * m - 1))
        pair = ((~r_odd & c_odd) if reverse else (r_odd & ~c_odd)) & same
        att = att + jnp.where(pair, gram, 0.0)
        up = pltpu.roll(tot, m, 0)
        dn = pltpu.roll(tot, c - m, 0)
        if reverse:
            qsum = qsum + jnp.where(odd, 0.0, dn)
            ksum = ksum + jnp.where(odd, up, 0.0)
        else:
            qsum = qsum + jnp.where(odd, up, 0.0)
            ksum = ksum + jnp.where(odd, 0.0, dn)
        tot = tot + jnp.where(odd, up, dn)
        m *= 2
    att = att + jnp.where(rowi == coli, jnp.sum(q * k, axis=-1, keepdims=True), 0.0)
    q_in = (q * jnp.exp(qsum)).astype(BF16)
    k_out = (k * jnp.exp(ksum)).astype(BF16)
    o = lax.dot_general(q_in, st.astype(BF16), _NT, preferred_element_type=F32)
    o = o + jnp.dot(att.astype(BF16), v, preferred_element_type=F32)
    st_new = st * jnp.exp(tot[0:1, :]) + lax.dot_general(v, k_out, _TN, preferred_element_type=F32)
    return o, st_new


def _rec_scan_kernel(q_ref, k_ref, lf_ref, v_ref, o_ref, st_ref, *, reverse):
    @pl.when(pl.program_id(1) == 0)
    def _():
        st_ref[...] = jnp.zeros_like(st_ref)

    n_chunks = SCAN_BLOCK // SCAN_CHUNK

    def body(ci, carry):
        cc = (n_chunks - 1 - ci) if reverse else ci
        rows = pl.ds(pl.multiple_of(cc * SCAN_CHUNK, SCAN_CHUNK), SCAN_CHUNK)
        for hd in range(REC_HEADS):
            cols = slice(hd * REC_EXPAND, (hd + 1) * REC_EXPAND)
            o, st_new = _scan_chunk(q_ref[0, rows, cols], k_ref[0, rows, cols], lf_ref[0, rows, cols],
                                    v_ref[0, rows, cols], st_ref[hd], reverse=reverse)
            o_ref[0, rows, cols] = o
            st_ref[hd] = st_new
        return carry

    lax.fori_loop(0, n_chunks, body, 0)


def _rec_scan(q, k, lf, v, *, reverse):
    b, s, _ = q.shape
    nblk = s // SCAN_BLOCK
    idx = (lambda bi, j: (bi, nblk - 1 - j, 0)) if reverse else (lambda bi, j: (bi, j, 0))
    blk = pl.BlockSpec((1, SCAN_BLOCK, D_MODEL), idx)
    return pl.pallas_call(
        functools.partial(_rec_scan_kernel, reverse=reverse),
        grid=(b, nblk),
        in_specs=[blk, blk, blk, blk],
        out_specs=blk,
        out_shape=jax.ShapeDtypeStruct((b, s, D_MODEL), F32),
        scratch_shapes=[pltpu.VMEM((REC_HEADS, REC_V_DIM, REC_EXPAND), F32)],
        compiler_params=pltpu.CompilerParams(dimension_semantics=("parallel", "arbitrary"),
                                             vmem_limit_bytes=VMEM_LIMIT),
        name="rec_scan_bwd" if reverse else "rec_scan_fwd",
    )(q, k, lf, v)


def _rec_out_kernel(of_ref, ob_ref, g_ref, gain_ref, h_ref, wo_ref, out_ref, y_ref):
    for hd in range(REC_HEADS):
        cols = slice(hd * REC_V_DIM, (hd + 1) * REC_V_DIM)
        o = of_ref[:, cols] + ob_ref[:, cols]
        g = g_ref[:, cols].astype(F32)
        y = o * _rms_scale(o) * gain_ref[...] * (g * jax.nn.sigmoid(g))
        y_ref[:, cols] = y.astype(BF16)
    out_ref[...] = h_ref[...] + jnp.dot(y_ref[...], wo_ref[...], preferred_element_type=F32)


def _rec_out(o_fw, o_bw, g, gain, h, w_out):
    t = h.shape[0]
    tm = min(REC_OUT_TM, t)
    row = pl.BlockSpec((tm, D_MODEL), lambda i: (i, 0))
    return pl.pallas_call(
        _rec_out_kernel,
        grid=(t // tm,),
        in_specs=[row, row, row, _const_spec((1, REC_V_DIM)), row, _const_spec((D_MODEL, D_MODEL))],
        out_specs=row,
        out_shape=jax.ShapeDtypeStruct((t, D_MODEL), F32),
        scratch_shapes=[pltpu.VMEM((tm, D_MODEL), BF16)],
        compiler_params=pltpu.CompilerParams(dimension_semantics=("parallel",), vmem_limit_bytes=VMEM_LIMIT),
        name="rec_out",
    )(o_fw, o_bw, g, gain, h, w_out)


def _trunk(x, p):
    b, s, d = x.shape
    t = b * s
    x = x.reshape(t, d)
    for layer in range(DEPTH):
        j = layer // N_MIXERS
        h, hn = _ffn(x, p["ffn1_norm"][layer], p["ffn1_w_gate"][layer], p["ffn1_w_up"][layer],
                     p["ffn1_w_down"][layer], p["mix_norm"][layer], final=False)
        if layer % N_MIXERS == 0:
            q, k, v = _attn_qkv(hn, p["attn_w_in"][j], p["attn_gq"][j], p["attn_gk"][j])
            h = _attn_core(q.reshape(b, s, d), k.reshape(b, s, KV_DIM), v.reshape(b, s, KV_DIM),
                           p["attn_sink"][j], h.reshape(b, s, d), p["attn_w_out"][j]).reshape(t, d)
        else:
            q, lf_fw, lf_bw, k_fw, k_bw, iv, g = _rec_proj(hn, p["rec_w_in"][j], p["lb"][j])
            r3 = lambda a: a.reshape(b, s, d)
            o_fw = _rec_scan(r3(q), r3(k_fw), r3(lf_fw), r3(iv), reverse=False)
            o_bw = _rec_scan(r3(q), r3(k_bw), r3(lf_bw), r3(iv), reverse=True)
            h = _rec_out(o_fw.reshape(t, d), o_bw.reshape(t, d), g, p["rec_o_gain"][j], h, p["rec_w_out"][j])
        x = _ffn(h, p["ffn2_norm"][layer], p["ffn2_w_gate"][layer], p["ffn2_w_up"][layer],
                 p["ffn2_w_down"][layer], p["out_norm"][layer], final=True)
    return x.reshape(b, s, d)


def kernel(x_prompt, x_sample, ffn1_norm, ffn1_w_gate, ffn1_w_up, ffn1_w_down, mix_norm, attn_w_in, attn_q_gain,
           attn_k_gain, attn_sink, attn_w_out, rec_w_in, rec_lb_logits, rec_o_gain, rec_w_out, ffn2_norm,
           ffn2_w_gate, ffn2_w_up, ffn2_w_down, out_norm):
    lb = jnp.cumsum(jax.nn.softmax(rec_lb_logits.astype(F32), axis=0), axis=0)
    lb = lb - lb[0:1]
    row = lambda a: a.astype(F32)[:, None, :]
    p = {
        "ffn1_norm": row(ffn1_norm), "mix_norm": row(mix_norm), "ffn2_norm": row(ffn2_norm),
        "out_norm": row(out_norm),
        "ffn1_w_gate": ffn1_w_gate.astype(BF16), "ffn1_w_up": ffn1_w_up.astype(BF16),
        "ffn1_w_down": ffn1_w_down.astype(BF16),
        "ffn2_w_gate": ffn2_w_gate.astype(BF16), "ffn2_w_up": ffn2_w_up.astype(BF16),
        "ffn2_w_down": ffn2_w_down.astype(BF16),
        "attn_w_in": attn_w_in.astype(BF16), "attn_w_out": attn_w_out.astype(BF16),
        "attn_gq": jnp.tile(attn_q_gain.astype(F32), (1, N_Q_HEADS))[:, None, :] * (HEAD_DIM ** -0.5),
        "attn_gk": jnp.tile(attn_k_gain.astype(F32), (1, N_KV_HEADS))[:, None, :],
        "attn_sink": attn_sink.astype(F32),
        "rec_w_in": rec_w_in.astype(BF16), "rec_w_out": rec_w_out.astype(BF16),
        "rec_o_gain": row(rec_o_gain), "lb": lb,
    }
    return (_trunk(x_prompt, p), _trunk(x_sample, p))
```

```python
import functools

import numpy as np
import jax
import jax.numpy as jnp
from jax import lax
from jax.experimental import pallas as pl
from jax.experimental.pallas import tpu as pltpu

D_MODEL = 1024
DEPTH = 4
N_MIXERS = 2
HEAD_DIM = 64
N_Q_HEADS = D_MODEL // HEAD_DIM
N_KV_HEADS = N_Q_HEADS // 4
GQA_GROUP = N_Q_HEADS // N_KV_HEADS
KV_DIM = N_KV_HEADS * HEAD_DIM
WINDOW = 128
ATTN_BLOCK = 128
REC_EXPAND = 128
REC_HEADS = D_MODEL // REC_EXPAND
REC_V_DIM = D_MODEL // REC_HEADS
D_FF = 2816
EPS = 1e-6

F32 = jnp.float32
BF16 = jnp.bfloat16

FFN_TM = 512
FFN_CHUNK = 256
QKV_TM = 512
REC_PROJ_TM = 256
REC_OUT_TM = 512
SCAN_CHUNK = 128
SCAN_BLOCK = 256
VMEM_LIMIT = 56 * 1024 * 1024

_NT = (((1,), (1,)), ((), ()))
_TN = (((0,), (0,)), ((), ()))


def _const_spec(shape):
    return pl.BlockSpec(shape, lambda *_: (0,) * len(shape))


def _rms_scale(x):
    return lax.rsqrt(jnp.mean(x * x, axis=-1, keepdims=True) + EPS)


def _ffn_kernel(x_ref, g1_ref, wg_ref, wu_ref, wd_ref, g2_ref, *rest, final):
    if final:
        y_ref, acc_ref = rest
    else:
        h_ref, hn_ref, acc_ref = rest
    x = x_ref[...]
    n = (x * _rms_scale(x) * g1_ref[...]).astype(BF16)
    for c in range(D_FF // FFN_CHUNK):
        sl = slice(c * FFN_CHUNK, (c + 1) * FFN_CHUNK)
        g = jnp.dot(n, wg_ref[:, sl], preferred_element_type=F32)
        u = jnp.dot(n, wu_ref[:, sl], preferred_element_type=F32)
        a = (g * jax.nn.sigmoid(g) * u).astype(BF16)
        d = jnp.dot(a, wd_ref[sl, :], preferred_element_type=F32)
        if c == 0:
            acc_ref[...] = d
        else:
            acc_ref[...] += d
    h = x + 0.5 * acc_ref[...]
    hn = h * _rms_scale(h) * g2_ref[...]
    if final:
        y_ref[...] = hn
    else:
        h_ref[...] = h
        hn_ref[...] = hn.astype(BF16)


def _ffn(x, g1, wg, wu, wd, g2, *, final):
    t = x.shape[0]
    tm = min(FFN_TM, t)
    row = pl.BlockSpec((tm, D_MODEL), lambda i: (i, 0))
    if final:
        out_shape = jax.ShapeDtypeStruct((t, D_MODEL), F32)
        out_specs = row
    else:
        out_shape = (jax.ShapeDtypeStruct((t, D_MODEL), F32), jax.ShapeDtypeStruct((t, D_MODEL), BF16))
        out_specs = (row, row)
    return pl.pallas_call(
        functools.partial(_ffn_kernel, final=final),
        grid=(t // tm,),
        in_specs=[row, _const_spec((1, D_MODEL)), _const_spec((D_MODEL, D_FF)), _const_spec((D_MODEL, D_FF)),
                  _const_spec((D_FF, D_MODEL)), _const_spec((1, D_MODEL))],
        out_specs=out_specs,
        out_shape=out_shape,
        scratch_shapes=[pltpu.VMEM((tm, D_MODEL), F32)],
        compiler_params=pltpu.CompilerParams(dimension_semantics=("parallel",), vmem_limit_bytes=VMEM_LIMIT),
        name="ffn_final" if final else "ffn_mid",
    )(x, g1, wg, wu, wd, g2)


def _group_mean_matrix(n, group):
    idx = np.arange(n) // group
    return jnp.asarray((idx[:, None] == idx[None, :]).astype(np.float32) / group, dtype=BF16)


def _qkv_kernel(hn_ref, w_ref, gq_ref, gk_ref, gm_ref, q_ref, k_ref, v_ref):
    hn = hn_ref[...]
    gm = gm_ref[...]
    qd = N_Q_HEADS * HEAD_DIM

    def normed(cols, gain):
        x = jnp.dot(hn, w_ref[:, cols], preferred_element_type=F32)
        ms = jnp.dot((x * x).astype(BF16), gm, preferred_element_type=F32)
        return (x * lax.rsqrt(ms + EPS) * gain).astype(BF16)

    for c in range(qd // KV_DIM):
        sl = slice(c * KV_DIM, (c + 1) * KV_DIM)
        q = normed(sl, gq_ref[:, sl])
        for j in range(N_KV_HEADS):
            q_ref[c * N_KV_HEADS + j] = q[:, j * HEAD_DIM:(j + 1) * HEAD_DIM]
    k = normed(slice(qd, qd + KV_DIM), gk_ref[...])
    for j in range(N_KV_HEADS):
        k_ref[j] = k[:, j * HEAD_DIM:(j + 1) * HEAD_DIM]
    v_ref[...] = jnp.dot(hn, w_ref[:, qd + KV_DIM:], preferred_element_type=F32).astype(BF16)


def _attn_qkv(hn, w_in, gq, gk):
    t = hn.shape[0]
    tm = min(QKV_TM, t)
    in_dim = w_in.shape[1]
    return pl.pallas_call(
        _qkv_kernel,
        grid=(t // tm,),
        in_specs=[pl.BlockSpec((tm, D_MODEL), lambda i: (i, 0)), _const_spec((D_MODEL, in_dim)),
                  _const_spec((1, D_MODEL)), _const_spec((1, KV_DIM)), _const_spec((KV_DIM, KV_DIM))],
        out_specs=(pl.BlockSpec((N_Q_HEADS, tm, HEAD_DIM), lambda i: (0, i, 0)),
                   pl.BlockSpec((N_KV_HEADS, tm, HEAD_DIM), lambda i: (0, i, 0)),
                   pl.BlockSpec((tm, KV_DIM), lambda i: (i, 0))),
        out_shape=(jax.ShapeDtypeStruct((N_Q_HEADS, t, HEAD_DIM), BF16),
                   jax.ShapeDtypeStruct((N_KV_HEADS, t, HEAD_DIM), BF16),
                   jax.ShapeDtypeStruct((t, KV_DIM), BF16)),
        compiler_params=pltpu.CompilerParams(dimension_semantics=("parallel",), vmem_limit_bytes=VMEM_LIMIT),
        name="attn_qkv",
    )(hn, w_in, gq, gk, _group_mean_matrix(KV_DIM, HEAD_DIM))


LOG2E = float(np.log2(np.e))
_ALIBI_SLOPES = [float(2.0 ** (-8.0 * (i + 1) / N_Q_HEADS)) for i in range(N_Q_HEADS)]
_GROUP_ORDER = (0, 2, 1, 3)
_MASKED = -1e30


def _per_head_rows(values, rows_per_head, shape):
    row = lax.broadcasted_iota(jnp.int32, shape, 0)
    out = jnp.full(shape, values[-1], F32)
    for j in range(len(values) - 2, -1, -1):
        out = jnp.where(row < (j + 1) * rows_per_head, values[j], out)
    return out


def _attn_core_kernel(sink_ref, q_ref, kp_ref, kc_ref, kn_ref, vp_ref, vc_ref, vn_ref, h_ref, wo_ref,
                      out_ref, att_ref, bias_ref, *, n_blocks):
    n = pl.program_id(1)
    blk = ATTN_BLOCK
    kb = 3 * blk
    gm = GQA_GROUP * blk

    @pl.when(n == 0)
    def _():
        row = lax.broadcasted_iota(jnp.int32, (gm, kb), 0)
        col = lax.broadcasted_iota(jnp.int32, (gm, kb), 1)
        absd = jnp.abs(col - WINDOW - (row & (blk - 1)))
        absd_f = absd.astype(F32)
        for g in range(N_KV_HEADS):
            slopes = [_ALIBI_SLOPES[g * GQA_GROUP + j] * LOG2E for j in _GROUP_ORDER]
            bias_ref[g] = jnp.where(absd <= WINDOW, -_per_head_rows(slopes, blk, (gm, kb)) * absd_f, _MASKED)

    col1 = lax.broadcasted_iota(jnp.int32, (1, kb), 1)
    edge = jnp.where(col1 < blk, jnp.where(n == 0, _MASKED, 0.0),
                     jnp.where(col1 >= 2 * blk, jnp.where(n == n_blocks - 1, _MASKED, 0.0), 0.0))
    lane = lax.broadcasted_iota(jnp.int32, (blk, 2 * HEAD_DIM), 1)
    ones = jnp.ones((kb, HEAD_DIM), BF16)
    for g in range(N_KV_HEADS):
        heads = [g * GQA_GROUP + j for j in _GROUP_ORDER]
        qg = jnp.concatenate([q_ref[hd, 0] for hd in heads], axis=0)
        kg = jnp.concatenate([kp_ref[g, 0], kc_ref[g, 0], kn_ref[g, 0]], axis=0)
        ks = slice(g * HEAD_DIM, (g + 1) * HEAD_DIM)
        vg = jnp.concatenate([vp_ref[0, :, ks], vc_ref[0, :, ks], vn_ref[0, :, ks]], axis=0)
        s = lax.dot_general(qg, kg, _NT, preferred_element_type=F32) + bias_ref[g] + edge
        sink = _per_head_rows([sink_ref[hd] * LOG2E for hd in heads], blk, (gm, 1))
        m = jnp.maximum(jnp.max(s, axis=-1, keepdims=True), sink)
        p = jnp.exp2(s - m).astype(BF16)
        sink_term = jnp.exp2(sink - m)
        half = gm // 2
        o_even = jnp.dot(p[:half], jnp.concatenate([vg, ones], axis=1), preferred_element_type=F32)
        o_odd = jnp.dot(p[half:], jnp.concatenate([ones, vg], axis=1), preferred_element_type=F32)
        a_even = o_even / (pltpu.roll(o_even, HEAD_DIM, 1) + sink_term[:half])
        a_odd = o_odd / (pltpu.roll(o_odd, HEAD_DIM, 1) + sink_term[half:])
        for pair in range(GQA_GROUP // 2):
            rows = slice(pair * blk, (pair + 1) * blk)
            cols = slice((g * GQA_GROUP + 2 * pair) * HEAD_DIM, (g * GQA_GROUP + 2 * pair + 2) * HEAD_DIM)
            att_ref[:, cols] = jnp.where(lane < HEAD_DIM, a_even[rows], a_odd[rows]).astype(BF16)
    out_ref[0] = h_ref[0] + jnp.dot(att_ref[...], wo_ref[...], preferred_element_type=F32)


def _attn_core(q, k, v, sink, h, w_out):
    _, b, s, _ = q.shape
    nb = s // ATTN_BLOCK
    cur = lambda bi, n: (bi, n, 0)
    hcur = lambda bi, n: (0, bi, n, 0)
    hprev = lambda bi, n: (0, bi, jnp.maximum(n - 1, 0), 0)
    hnext = lambda bi, n: (0, bi, jnp.minimum(n + 1, nb - 1), 0)
    prev = lambda bi, n: (bi, jnp.maximum(n - 1, 0), 0)
    nxt = lambda bi, n: (bi, jnp.minimum(n + 1, nb - 1), 0)
    q_blk = (N_Q_HEADS, 1, ATTN_BLOCK, HEAD_DIM)
    k_blk = (N_KV_HEADS, 1, ATTN_BLOCK, HEAD_DIM)
    v_blk = (1, ATTN_BLOCK, KV_DIM)
    d_blk = (1, ATTN_BLOCK, D_MODEL)
    return pl.pallas_call(
        functools.partial(_attn_core_kernel, n_blocks=nb),
        grid=(b, nb),
        in_specs=[pl.BlockSpec(memory_space=pltpu.SMEM),
                  pl.BlockSpec(q_blk, hcur),
                  pl.BlockSpec(k_blk, hprev), pl.BlockSpec(k_blk, hcur), pl.BlockSpec(k_blk, hnext),
                  pl.BlockSpec(v_blk, prev), pl.BlockSpec(v_blk, cur), pl.BlockSpec(v_blk, nxt),
                  pl.BlockSpec(d_blk, cur), _const_spec((D_MODEL, D_MODEL))],
        out_specs=pl.BlockSpec(d_blk, cur),
        out_shape=jax.ShapeDtypeStruct((b, s, D_MODEL), F32),
        scratch_shapes=[pltpu.VMEM((ATTN_BLOCK, D_MODEL), BF16),
                        pltpu.VMEM((N_KV_HEADS, GQA_GROUP * ATTN_BLOCK, 3 * ATTN_BLOCK), F32)],
        compiler_params=pltpu.CompilerParams(dimension_semantics=("arbitrary", "arbitrary"),
                                             vmem_limit_bytes=VMEM_LIMIT),
        name="attn_core",
    )(sink, q, k, k, k, v, v, v, h, w_out)


def _rec_proj_kernel(hn_ref, w_ref, wit_ref, lb_ref, q_ref, lf_fw_ref, lf_bw_ref, k_fw_ref, k_bw_ref, i_ref, it_ref,
                     g_ref):
    hn = hn_ref[...]
    f = D_MODEL
    q_ref[...] = jnp.dot(hn, w_ref[:, 0:f], preferred_element_type=F32)
    for d, (lf_ref, k_ref) in enumerate(((lf_fw_ref, k_fw_ref), (lf_bw_ref, k_bw_ref))):
        z = jnp.dot(hn, w_ref[:, (1 + d) * f:(2 + d) * f], preferred_element_type=F32)
        lb = lb_ref[d:d + 1, :]
        log_lb = jnp.log(lb)
        log_1m_lb = jnp.log1p(-lb)
        e = jnp.exp(-jnp.abs(z))
        ope = 1.0 + e
        c = log_1m_lb + (jnp.minimum(z, 0.0) - jnp.log(ope))
        lf_ref[...] = jnp.maximum(log_lb, c) + jnp.log(1.0 + jnp.exp(-jnp.abs(log_lb - c)))
        k_ref[...] = (1.0 - lb) * (jnp.where(z >= 0.0, e, 1.0) / ope)
    i_ref[...] = jnp.dot(hn, w_ref[:, 3 * f:4 * f], preferred_element_type=F32).astype(BF16)
    it_ref[...] = lax.dot_general(wit_ref[...], hn, _NT, preferred_element_type=F32).astype(BF16)
    g_ref[...] = jnp.dot(hn, w_ref[:, 4 * f:5 * f], preferred_element_type=F32).astype(BF16)


def _rec_proj(hn, w_in, w_it, lb):
    t = hn.shape[0]
    tm = min(REC_PROJ_TM, t)
    row = pl.BlockSpec((tm, D_MODEL), lambda i: (i, 0))
    f32o = jax.ShapeDtypeStruct((t, D_MODEL), F32)
    bf16o = jax.ShapeDtypeStruct((t, D_MODEL), BF16)
    return pl.pallas_call(
        _rec_proj_kernel,
        grid=(t // tm,),
        in_specs=[row, _const_spec((D_MODEL, w_in.shape[1])), _const_spec((D_MODEL, D_MODEL)),
                  _const_spec((2, D_MODEL))],
        out_specs=(row,) * 6 + (pl.BlockSpec((D_MODEL, tm), lambda i: (0, i)), row),
        out_shape=(f32o, f32o, f32o, f32o, f32o, bf16o, jax.ShapeDtypeStruct((D_MODEL, t), BF16), bf16o),
        compiler_params=pltpu.CompilerParams(dimension_semantics=("parallel",), vmem_limit_bytes=VMEM_LIMIT),
        name="rec_proj",
    )(hn, w_in, w_it, lb)


def _scan_chunk(q, k, lf, v, vt, st, *, reverse):
    c = q.shape[0]
    ridx = lax.broadcasted_iota(jnp.int32, q.shape, 0)
    rowi = lax.broadcasted_iota(jnp.int32, (c, c), 0)
    coli = lax.broadcasted_iota(jnp.int32, (c, c), 1)
    tot = lf
    qsum = lf
    ksum = jnp.zeros_like(lf)
    att = jnp.where(rowi == coli, jnp.sum(q * k, axis=-1, keepdims=True), 0.0)
    m = 1
    while m < c:
        odd = (ridx & m) != 0
        q_rows = ~odd if reverse else odd
        z = jnp.where(q_rows, qsum, ksum)
        x = (jnp.where(q_rows, q, k) * jnp.exp(z)).astype(BF16)
        gram = lax.dot_general(x, x, _NT, preferred_element_type=F32)
        r_odd = (rowi & m) != 0
        c_odd = (coli & m) != 0
        same = (rowi & ~(2 * m - 1)) == (coli & ~(2 * m - 1))
        pair = ((~r_odd & c_odd) if reverse else (r_odd & ~c_odd)) & same
        att = jnp.where(pair, gram, att)
        up = pltpu.roll(tot, m, 0)
        dn = pltpu.roll(tot, c - m, 0)
        if reverse:
            qsum = qsum + jnp.where(odd, 0.0, dn)
            ksum = ksum + jnp.where(odd, up, 0.0)
        else:
            qsum = qsum + jnp.where(odd, up, 0.0)
            ksum = ksum + jnp.where(odd, 0.0, dn)
        tot = tot + jnp.where(odd, up, dn)
        m *= 2
    q_in = (q * jnp.exp(qsum)).astype(BF16)
    k_out = (k * jnp.exp(ksum)).astype(BF16)
    o = lax.dot_general(q_in, st.astype(BF16), _NT, preferred_element_type=F32)
    o = o + jnp.dot(att.astype(BF16), v, preferred_element_type=F32)
    st_new = st * jnp.exp(tot[0:1, :]) + jnp.dot(vt, k_out, preferred_element_type=F32)
    return o, st_new


def _rec_scan_kernel(q_ref, k_ref, lf_ref, v_ref, vt_ref, o_ref, st_ref, *, reverse):
    @pl.when(pl.program_id(1) == 0)
    def _():
        st_ref[...] = jnp.zeros_like(st_ref)

    n_chunks = SCAN_BLOCK // SCAN_CHUNK
    for cc in (reversed(range(n_chunks)) if reverse else range(n_chunks)):
        rows = slice(cc * SCAN_CHUNK, (cc + 1) * SCAN_CHUNK)
        for hd in range(REC_HEADS):
            cols = slice(hd * REC_EXPAND, (hd + 1) * REC_EXPAND)
            o, st_new = _scan_chunk(q_ref[0, rows, cols], k_ref[0, rows, cols], lf_ref[0, rows, cols],
                                    v_ref[0, rows, cols], vt_ref[cols, rows], st_ref[hd], reverse=reverse)
            o_ref[0, rows, cols] = o
            st_ref[hd] = st_new


def _rec_scan(q, k, lf, v, vt, *, reverse):
    b, s, _ = q.shape
    nblk = s // SCAN_BLOCK
    blk_of = (lambda j: nblk - 1 - j) if reverse else (lambda j: j)
    blk = pl.BlockSpec((1, SCAN_BLOCK, D_MODEL), lambda bi, j: (bi, blk_of(j), 0))
    vt_blk = pl.BlockSpec((D_MODEL, SCAN_BLOCK), lambda bi, j: (0, bi * nblk + blk_of(j)))
    return pl.pallas_call(
        functools.partial(_rec_scan_kernel, reverse=reverse),
        grid=(b, nblk),
        in_specs=[blk, blk, blk, blk, vt_blk],
        out_specs=blk,
        out_shape=jax.ShapeDtypeStruct((b, s, D_MODEL), F32),
        scratch_shapes=[pltpu.VMEM((REC_HEADS, REC_V_DIM, REC_EXPAND), F32)],
        compiler_params=pltpu.CompilerParams(dimension_semantics=("parallel", "arbitrary"),
                                             vmem_limit_bytes=VMEM_LIMIT),
        name="rec_scan_bwd" if reverse else "rec_scan_fwd",
    )(q, k, lf, v, vt)


def _rec_out_kernel(of_ref, ob_ref, g_ref, gain_ref, h_ref, wo_ref, out_ref, y_ref):
    for hd in range(REC_HEADS):
        cols = slice(hd * REC_V_DIM, (hd + 1) * REC_V_DIM)
        o = of_ref[:, cols] + ob_ref[:, cols]
        g = g_ref[:, cols].astype(F32)
        y = o * _rms_scale(o) * gain_ref[...] * (g * jax.nn.sigmoid(g))
        y_ref[:, cols] = y.astype(BF16)
    out_ref[...] = h_ref[...] + jnp.dot(y_ref[...], wo_ref[...], preferred_element_type=F32)


def _rec_out(o_fw, o_bw, g, gain, h, w_out):
    t = h.shape[0]
    tm = min(REC_OUT_TM, t)
    row = pl.BlockSpec((tm, D_MODEL), lambda i: (i, 0))
    return pl.pallas_call(
        _rec_out_kernel,
        grid=(t // tm,),
        in_specs=[row, row, row, _const_spec((1, REC_V_DIM)), row, _const_spec((D_MODEL, D_MODEL))],
        out_specs=row,
        out_shape=jax.ShapeDtypeStruct((t, D_MODEL), F32),
        scratch_shapes=[pltpu.VMEM((tm, D_MODEL), BF16)],
        compiler_params=pltpu.CompilerParams(dimension_semantics=("parallel",), vmem_limit_bytes=VMEM_LIMIT),
        name="rec_out",
    )(o_fw, o_bw, g, gain, h, w_out)


def _trunk(x, p):
    b, s, d = x.shape
    t = b * s
    x = x.reshape(t, d)
    for layer in range(DEPTH):
        j = layer // N_MIXERS
        h, hn = _ffn(x, p["ffn1_norm"][layer], p["ffn1_w_gate"][layer], p["ffn1_w_up"][layer],
                     p["ffn1_w_down"][layer], p["mix_norm"][layer], final=False)
        if layer % N_MIXERS == 0:
            q, k, v = _attn_qkv(hn, p["attn_w_in"][j], p["attn_gq"][j], p["attn_gk"][j])
            h = _attn_core(q.reshape(N_Q_HEADS, b, s, HEAD_DIM), k.reshape(N_KV_HEADS, b, s, HEAD_DIM),
                           v.reshape(b, s, KV_DIM), p["attn_sink"][j], h.reshape(b, s, d),
                           p["attn_w_out"][j]).reshape(t, d)
        else:
            q, lf_fw, lf_bw, k_fw, k_bw, iv, ivt, g = _rec_proj(hn, p["rec_w_in"][j], p["rec_w_it"][j], p["lb"][j])
            r3 = lambda a: a.reshape(b, s, d)
            o_fw = _rec_scan(r3(q), r3(k_fw), r3(lf_fw), r3(iv), ivt, reverse=False)
            o_bw = _rec_scan(r3(q), r3(k_bw), r3(lf_bw), r3(iv), ivt, reverse=True)
            h = _rec_out(o_fw.reshape(t, d), o_bw.reshape(t, d), g, p["rec_o_gain"][j], h, p["rec_w_out"][j])
        x = _ffn(h, p["ffn2_norm"][layer], p["ffn2_w_gate"][layer], p["ffn2_w_up"][layer],
                 p["ffn2_w_down"][layer], p["out_norm"][layer], final=True)
    return x.reshape(b, s, d)


def kernel(x_prompt, x_sample, ffn1_norm, ffn1_w_gate, ffn1_w_up, ffn1_w_down, mix_norm, attn_w_in, attn_q_gain,
           attn_k_gain, attn_sink, attn_w_out, rec_w_in, rec_lb_logits, rec_o_gain, rec_w_out, ffn2_norm,
           ffn2_w_gate, ffn2_w_up, ffn2_w_down, out_norm):
    lb = jnp.cumsum(jax.nn.softmax(rec_lb_logits.astype(F32), axis=0), axis=0)
    lb = lb - lb[0:1]
    row = lambda a: a.astype(F32)[:, None, :]
    p = {
        "ffn1_norm": row(ffn1_norm), "mix_norm": row(mix_norm), "ffn2_norm": row(ffn2_norm),
        "out_norm": row(out_norm),
        "ffn1_w_gate": ffn1_w_gate.astype(BF16), "ffn1_w_up": ffn1_w_up.astype(BF16),
        "ffn1_w_down": ffn1_w_down.astype(BF16),
        "ffn2_w_gate": ffn2_w_gate.astype(BF16), "ffn2_w_up": ffn2_w_up.astype(BF16),
        "ffn2_w_down": ffn2_w_down.astype(BF16),
        "attn_w_in": attn_w_in.astype(BF16), "attn_w_out": attn_w_out.astype(BF16),
        "attn_gq": jnp.tile(attn_q_gain.astype(F32), (1, N_Q_HEADS))[:, None, :] * (HEAD_DIM ** -0.5 * LOG2E),
        "attn_gk": jnp.tile(attn_k_gain.astype(F32), (1, N_KV_HEADS))[:, None, :],
        "attn_sink": attn_sink.astype(F32),
        "rec_w_in": rec_w_in.astype(BF16), "rec_w_out": rec_w_out.astype(BF16),
        "rec_w_it": jnp.swapaxes(rec_w_in[:, :, 3 * D_MODEL:4 * D_MODEL], 1, 2).astype(BF16),
        "rec_o_gain": row(rec_o_gain), "lb": lb,
    }
    return (_trunk(x_prompt, p), _trunk(x_sample, p))
```

```python
import functools

import numpy as np
import jax
import jax.numpy as jnp
from jax import lax
from jax.experimental import pallas as pl
from jax.experimental.pallas import tpu as pltpu

D_MODEL = 1024
DEPTH = 4
N_MIXERS = 2
HEAD_DIM = 64
N_Q_HEADS = D_MODEL // HEAD_DIM
N_KV_HEADS = N_Q_HEADS // 4
GQA_GROUP = N_Q_HEADS // N_KV_HEADS
KV_DIM = N_KV_HEADS * HEAD_DIM
WINDOW = 128
ATTN_BLOCK = 128
REC_EXPAND = 128
REC_HEADS = D_MODEL // REC_EXPAND
REC_V_DIM = D_MODEL // REC_HEADS
D_FF = 2816
EPS = 1e-6

F32 = jnp.float32
BF16 = jnp.bfloat16

FFN_TM = 512
FFN_CHUNK = 256
QKV_TM = 512
REC_PROJ_TM = 256
REC_PROJ_CHUNK = 256
SCAN_CHUNK = 128
SCAN_BLOCK = 256
VMEM_LIMIT = 56 * 1024 * 1024

_NT = (((1,), (1,)), ((), ()))
_TN = (((0,), (0,)), ((), ()))


def _const_spec(shape):
    return pl.BlockSpec(shape, lambda *_: (0,) * len(shape))


def _rms_scale(x):
    return lax.rsqrt(jnp.mean(x * x, axis=-1, keepdims=True) + EPS)


def _ffn_kernel(x_ref, g1_ref, wg_ref, wu_ref, wd_ref, g2_ref, *rest, final):
    if final:
        y_ref, acc_ref = rest
    else:
        h_ref, hn_ref, acc_ref = rest
    x = x_ref[...]
    n = (x * _rms_scale(x) * g1_ref[...]).astype(BF16)
    for c in range(D_FF // FFN_CHUNK):
        sl = slice(c * FFN_CHUNK, (c + 1) * FFN_CHUNK)
        g = jnp.dot(n, wg_ref[:, sl], preferred_element_type=F32)
        u = jnp.dot(n, wu_ref[:, sl], preferred_element_type=F32)
        a = (g * jax.nn.sigmoid(g) * u).astype(BF16)
        d = jnp.dot(a, wd_ref[sl, :], preferred_element_type=F32)
        if c == 0:
            acc_ref[...] = d
        else:
            acc_ref[...] += d
    h = x + 0.5 * acc_ref[...]
    hn = h * _rms_scale(h) * g2_ref[...]
    if final:
        y_ref[...] = hn
    else:
        h_ref[...] = h
        hn_ref[...] = hn.astype(BF16)


def _ffn(x, g1, wg, wu, wd, g2, *, final):
    t = x.shape[0]
    tm = min(FFN_TM, t)
    row = pl.BlockSpec((tm, D_MODEL), lambda i: (i, 0))
    if final:
        out_shape = jax.ShapeDtypeStruct((t, D_MODEL), F32)
        out_specs = row
    else:
        out_shape = (jax.ShapeDtypeStruct((t, D_MODEL), F32), jax.ShapeDtypeStruct((t, D_MODEL), BF16))
        out_specs = (row, row)
    return pl.pallas_call(
        functools.partial(_ffn_kernel, final=final),
        grid=(t // tm,),
        in_specs=[row, _const_spec((1, D_MODEL)), _const_spec((D_MODEL, D_FF)), _const_spec((D_MODEL, D_FF)),
                  _const_spec((D_FF, D_MODEL)), _const_spec((1, D_MODEL))],
        out_specs=out_specs,
        out_shape=out_shape,
        scratch_shapes=[pltpu.VMEM((tm, D_MODEL), F32)],
        compiler_params=pltpu.CompilerParams(dimension_semantics=("parallel",), vmem_limit_bytes=VMEM_LIMIT),
        name="ffn_final" if final else "ffn_mid",
    )(x, g1, wg, wu, wd, g2)


def _group_mean_matrix(n, group):
    idx = np.arange(n) // group
    return jnp.asarray((idx[:, None] == idx[None, :]).astype(np.float32) / group, dtype=BF16)


def _qkv_kernel(hn_ref, w_ref, gq_ref, gk_ref, gm_ref, q_ref, k_ref, v_ref):
    hn = hn_ref[...]
    gm = gm_ref[...]
    qd = N_Q_HEADS * HEAD_DIM

    def normed(cols, gain):
        x = jnp.dot(hn, w_ref[:, cols], preferred_element_type=F32)
        ms = jnp.dot((x * x).astype(BF16), gm, preferred_element_type=F32)
        return (x * lax.rsqrt(ms + EPS) * gain).astype(BF16)

    for c in range(qd // KV_DIM):
        sl = slice(c * KV_DIM, (c + 1) * KV_DIM)
        q = normed(sl, gq_ref[:, sl])
        for j in range(N_KV_HEADS):
            for rb in range(q_ref.shape[0]):
                q_ref[rb, c * N_KV_HEADS + j] = q[rb * ATTN_BLOCK:(rb + 1) * ATTN_BLOCK,
                                                  j * HEAD_DIM:(j + 1) * HEAD_DIM]
    k = normed(slice(qd, qd + KV_DIM), gk_ref[...])
    for j in range(N_KV_HEADS):
        for rb in range(k_ref.shape[0]):
            k_ref[rb, j] = k[rb * ATTN_BLOCK:(rb + 1) * ATTN_BLOCK, j * HEAD_DIM:(j + 1) * HEAD_DIM]
    v_ref[...] = jnp.dot(hn, w_ref[:, qd + KV_DIM:], preferred_element_type=F32).astype(BF16)


def _attn_qkv(hn, w_in, gq, gk):
    t = hn.shape[0]
    tm = min(QKV_TM, t)
    nrb = tm // ATTN_BLOCK
    in_dim = w_in.shape[1]
    return pl.pallas_call(
        _qkv_kernel,
        grid=(t // tm,),
        in_specs=[pl.BlockSpec((tm, D_MODEL), lambda i: (i, 0)), _const_spec((D_MODEL, in_dim)),
                  _const_spec((1, D_MODEL)), _const_spec((1, KV_DIM)), _const_spec((KV_DIM, KV_DIM))],
        out_specs=(pl.BlockSpec((nrb, N_Q_HEADS, ATTN_BLOCK, HEAD_DIM), lambda i: (i, 0, 0, 0)),
                   pl.BlockSpec((nrb, N_KV_HEADS, ATTN_BLOCK, HEAD_DIM), lambda i: (i, 0, 0, 0)),
                   pl.BlockSpec((tm, KV_DIM), lambda i: (i, 0))),
        out_shape=(jax.ShapeDtypeStruct((t // ATTN_BLOCK, N_Q_HEADS, ATTN_BLOCK, HEAD_DIM), BF16),
                   jax.ShapeDtypeStruct((t // ATTN_BLOCK, N_KV_HEADS, ATTN_BLOCK, HEAD_DIM), BF16),
                   jax.ShapeDtypeStruct((t, KV_DIM), BF16)),
        compiler_params=pltpu.CompilerParams(dimension_semantics=("parallel",), vmem_limit_bytes=VMEM_LIMIT),
        name="attn_qkv",
    )(hn, w_in, gq, gk, _group_mean_matrix(KV_DIM, HEAD_DIM))


LOG2E = float(np.log2(np.e))
_ALIBI_SLOPES = [float(2.0 ** (-8.0 * (i + 1) / N_Q_HEADS)) for i in range(N_Q_HEADS)]
_GROUP_ORDER = (0, 2, 1, 3)
_MASKED = -1e30


def _per_head_rows(values, rows_per_head, shape):
    row = lax.broadcasted_iota(jnp.int32, shape, 0)
    out = jnp.full(shape, values[-1], F32)
    for j in range(len(values) - 2, -1, -1):
        out = jnp.where(row < (j + 1) * rows_per_head, values[j], out)
    return out


def _attn_core_kernel(sink_ref, q_ref, kp_ref, kc_ref, kn_ref, vp_ref, vc_ref, vn_ref, h_ref, wo_ref,
                      out_ref, att_ref, bias_ref, *, n_blocks):
    n = pl.program_id(1)
    blk = ATTN_BLOCK
    kb = 3 * blk
    gm = GQA_GROUP * blk

    @pl.when(n == 0)
    def _():
        row = lax.broadcasted_iota(jnp.int32, (gm, kb), 0)
        col = lax.broadcasted_iota(jnp.int32, (gm, kb), 1)
        absd = jnp.abs(col - WINDOW - (row & (blk - 1)))
        absd_f = absd.astype(F32)
        for g in range(N_KV_HEADS):
            slopes = [_ALIBI_SLOPES[g * GQA_GROUP + j] * LOG2E for j in _GROUP_ORDER]
            bias_ref[g] = jnp.where(absd <= WINDOW, -_per_head_rows(slopes, blk, (gm, kb)) * absd_f, _MASKED)

    col1 = lax.broadcasted_iota(jnp.int32, (1, kb), 1)
    edge = jnp.where(col1 < blk, jnp.where(n == 0, _MASKED, 0.0),
                     jnp.where(col1 >= 2 * blk, jnp.where(n == n_blocks - 1, _MASKED, 0.0), 0.0))
    lane = lax.broadcasted_iota(jnp.int32, (blk, 2 * HEAD_DIM), 1)
    ones = jnp.ones((kb, HEAD_DIM), BF16)
    for g in range(N_KV_HEADS):
        heads = [g * GQA_GROUP + j for j in _GROUP_ORDER]
        qg = jnp.concatenate([q_ref[0, hd] for hd in heads], axis=0)
        kg = jnp.concatenate([kp_ref[0, g], kc_ref[0, g], kn_ref[0, g]], axis=0)
        ks = slice(g * HEAD_DIM, (g + 1) * HEAD_DIM)
        vg = jnp.concatenate([vp_ref[0, :, ks], vc_ref[0, :, ks], vn_ref[0, :, ks]], axis=0)
        s = lax.dot_general(qg, kg, _NT, preferred_element_type=F32) + bias_ref[g] + edge
        sink = _per_head_rows([sink_ref[hd] * LOG2E for hd in heads], blk, (gm, 1))
        m = jnp.maximum(jnp.max(s, axis=-1, keepdims=True), sink)
        p = jnp.exp2(s - m).astype(BF16)
        sink_term = jnp.exp2(sink - m)
        half = gm // 2
        o_even = jnp.dot(p[:half], jnp.concatenate([vg, ones], axis=1), preferred_element_type=F32)
        o_odd = jnp.dot(p[half:], jnp.concatenate([ones, vg], axis=1), preferred_element_type=F32)
        a_even = o_even / (pltpu.roll(o_even, HEAD_DIM, 1) + sink_term[:half])
        a_odd = o_odd / (pltpu.roll(o_odd, HEAD_DIM, 1) + sink_term[half:])
        for pair in range(GQA_GROUP // 2):
            rows = slice(pair * blk, (pair + 1) * blk)
            cols = slice((g * GQA_GROUP + 2 * pair) * HEAD_DIM, (g * GQA_GROUP + 2 * pair + 2) * HEAD_DIM)
            att_ref[:, cols] = jnp.where(lane < HEAD_DIM, a_even[rows], a_odd[rows]).astype(BF16)
    out_ref[0] = h_ref[0] + jnp.dot(att_ref[...], wo_ref[...], preferred_element_type=F32)


def _attn_core(q, k, v, sink, h, w_out):
    b, s, _ = h.shape
    nb = s // ATTN_BLOCK
    cur = lambda bi, n: (bi, n, 0)
    hcur = lambda bi, n: (bi * nb + n, 0, 0, 0)
    hprev = lambda bi, n: (bi * nb + jnp.maximum(n - 1, 0), 0, 0, 0)
    hnext = lambda bi, n: (bi * nb + jnp.minimum(n + 1, nb - 1), 0, 0, 0)
    prev = lambda bi, n: (bi, jnp.maximum(n - 1, 0), 0)
    nxt = lambda bi, n: (bi, jnp.minimum(n + 1, nb - 1), 0)
    q_blk = (1, N_Q_HEADS, ATTN_BLOCK, HEAD_DIM)
    k_blk = (1, N_KV_HEADS, ATTN_BLOCK, HEAD_DIM)
    v_blk = (1, ATTN_BLOCK, KV_DIM)
    d_blk = (1, ATTN_BLOCK, D_MODEL)
    return pl.pallas_call(
        functools.partial(_attn_core_kernel, n_blocks=nb),
        grid=(b, nb),
        in_specs=[pl.BlockSpec(memory_space=pltpu.SMEM),
                  pl.BlockSpec(q_blk, hcur),
                  pl.BlockSpec(k_blk, hprev), pl.BlockSpec(k_blk, hcur), pl.BlockSpec(k_blk, hnext),
                  pl.BlockSpec(v_blk, prev), pl.BlockSpec(v_blk, cur), pl.BlockSpec(v_blk, nxt),
                  pl.BlockSpec(d_blk, cur), _const_spec((D_MODEL, D_MODEL))],
        out_specs=pl.BlockSpec(d_blk, cur),
        out_shape=jax.ShapeDtypeStruct((b, s, D_MODEL), F32),
        scratch_shapes=[pltpu.VMEM((ATTN_BLOCK, D_MODEL), BF16),
                        pltpu.VMEM((N_KV_HEADS, GQA_GROUP * ATTN_BLOCK, 3 * ATTN_BLOCK), F32)],
        compiler_params=pltpu.CompilerParams(dimension_semantics=("arbitrary", "arbitrary"),
                                             vmem_limit_bytes=VMEM_LIMIT),
        name="attn_core",
    )(sink, q, k, k, k, v, v, v, h, w_out)


def _rec_proj_kernel(hn_ref, w_ref, wit_ref, lb_ref, q_ref, lf_fw_ref, lf_bw_ref, k_fw_ref, k_bw_ref, i_ref, it_ref,
                     g_ref):
    hn = hn_ref[...]
    f = D_MODEL
    cw = REC_PROJ_CHUNK

    def proj(col):
        return jnp.dot(hn, w_ref[:, col:col + cw], preferred_element_type=F32)

    for c in range(f // cw):
        sl = slice(c * cw, (c + 1) * cw)
        q_ref[:, sl] = proj(c * cw)
        for d, (lf_ref, k_ref) in enumerate(((lf_fw_ref, k_fw_ref), (lf_bw_ref, k_bw_ref))):
            z2 = proj((1 + d) * f + c * cw) * LOG2E
            lb = lb_ref[d:d + 1, sl]
            log_lb = jnp.log2(lb)
            log_1m_lb = jnp.log1p(-lb) * LOG2E
            e = jnp.exp2(-jnp.abs(z2))
            ope = 1.0 + e
            cl = log_1m_lb + (jnp.minimum(z2, 0.0) - jnp.log2(ope))
            lf_ref[:, sl] = jnp.maximum(log_lb, cl) + jnp.log2(1.0 + jnp.exp2(-jnp.abs(log_lb - cl)))
            k_ref[:, sl] = (1.0 - lb) * (jnp.where(z2 >= 0.0, e, 1.0) / ope)
        i_ref[:, sl] = proj(3 * f + c * cw).astype(BF16)
        it_ref[0, sl, :] = lax.dot_general(wit_ref[sl, :], hn, _NT, preferred_element_type=F32).astype(BF16)
        g_ref[:, sl] = proj(4 * f + c * cw).astype(BF16)


def _rec_proj(hn, w_in, w_it, lb):
    t = hn.shape[0]
    tm = REC_PROJ_TM
    row = pl.BlockSpec((tm, D_MODEL), lambda i: (i, 0))
    f32o = jax.ShapeDtypeStruct((t, D_MODEL), F32)
    bf16o = jax.ShapeDtypeStruct((t, D_MODEL), BF16)
    return pl.pallas_call(
        _rec_proj_kernel,
        grid=(t // tm,),
        in_specs=[row, _const_spec((D_MODEL, w_in.shape[1])), _const_spec((D_MODEL, D_MODEL)),
                  _const_spec((2, D_MODEL))],
        out_specs=(row,) * 6 + (pl.BlockSpec((1, D_MODEL, tm), lambda i: (i, 0, 0)), row),
        out_shape=(f32o, f32o, f32o, f32o, f32o, bf16o, jax.ShapeDtypeStruct((t // tm, D_MODEL, tm), BF16), bf16o),
        compiler_params=pltpu.CompilerParams(dimension_semantics=("parallel",), vmem_limit_bytes=VMEM_LIMIT),
        name="rec_proj",
    )(hn, w_in, w_it, lb)


def _scan_chunk(q, k, lf, v, vt, st, *, reverse):
    c = q.shape[0]
    ridx = lax.broadcasted_iota(jnp.int32, q.shape, 0)
    rowi = lax.broadcasted_iota(jnp.int32, (c, c), 0)
    coli = lax.broadcasted_iota(jnp.int32, (c, c), 1)
    tot = lf
    qsum = lf
    ksum = jnp.zeros_like(lf)
    att = jnp.where(rowi == coli, jnp.sum(q * k, axis=-1, keepdims=True), 0.0)
    m = 1
    while m < c:
        odd = (ridx & m) != 0
        q_rows = ~odd if reverse else odd
        z = jnp.where(q_rows, qsum, ksum)
        x = (jnp.where(q_rows, q, k) * jnp.exp2(z)).astype(BF16)
        gram = lax.dot_general(x, x, _NT, preferred_element_type=F32)
        r_odd = (rowi & m) != 0
        c_odd = (coli & m) != 0
        same = (rowi & ~(2 * m - 1)) == (coli & ~(2 * m - 1))
        pair = ((~r_odd & c_odd) if reverse else (r_odd & ~c_odd)) & same
        att = jnp.where(pair, gram, att)
        up = pltpu.roll(tot, m, 0)
        dn = pltpu.roll(tot, c - m, 0)
        if reverse:
            qsum = qsum + jnp.where(odd, 0.0, dn)
            ksum = ksum + jnp.where(odd, up, 0.0)
        else:
            qsum = qsum + jnp.where(odd, up, 0.0)
            ksum = ksum + jnp.where(odd, 0.0, dn)
        tot = tot + jnp.where(odd, up, dn)
        m *= 2
    q_in = (q * jnp.exp2(qsum)).astype(BF16)
    k_out = (k * jnp.exp2(ksum)).astype(BF16)
    o = lax.dot_general(q_in, st.astype(BF16), _NT, preferred_element_type=F32)
    o = o + jnp.dot(att.astype(BF16), v, preferred_element_type=F32)
    st_new = st * jnp.exp2(tot[0:1, :]) + jnp.dot(vt, k_out, preferred_element_type=F32)
    return o, st_new


def _rec_scan_kernel(*refs, reverse):
    if reverse:
        (q_ref, k_ref, lf_ref, v_ref, vt_ref, of_ref, g_ref, gain_ref, h_ref, wo_ref, out_ref, st_ref,
         y_ref) = refs
    else:
        q_ref, k_ref, lf_ref, v_ref, vt_ref, of_ref, st_ref = refs

    @pl.when(pl.program_id(1) == 0)
    def _():
        st_ref[...] = jnp.zeros_like(st_ref)

    n_chunks = SCAN_BLOCK // SCAN_CHUNK
    for cc in (reversed(range(n_chunks)) if reverse else range(n_chunks)):
        rows = slice(cc * SCAN_CHUNK, (cc + 1) * SCAN_CHUNK)
        for hd in range(REC_HEADS):
            cols = slice(hd * REC_EXPAND, (hd + 1) * REC_EXPAND)
            o, st_new = _scan_chunk(q_ref[0, rows, cols], k_ref[0, rows, cols], lf_ref[0, rows, cols],
                                    v_ref[0, rows, cols], vt_ref[0, cols, rows], st_ref[hd], reverse=reverse)
            st_ref[hd] = st_new
            if reverse:
                o = o + of_ref[0, rows, cols].astype(F32)
                g = g_ref[0, rows, cols].astype(F32)
                y = o * _rms_scale(o) * gain_ref[...] * (g * jax.nn.sigmoid(g))
                y_ref[rows, cols] = y.astype(BF16)
            else:
                of_ref[0, rows, cols] = o.astype(BF16)
    if reverse:
        out_ref[0] = h_ref[0] + jnp.dot(y_ref[...], wo_ref[...], preferred_element_type=F32)


def _rec_scan(q, k, lf, v, vt, tail=None, *, reverse):
    b, s, _ = q.shape
    nblk = s // SCAN_BLOCK
    blk_of = (lambda j: nblk - 1 - j) if reverse else (lambda j: j)
    blk = pl.BlockSpec((1, SCAN_BLOCK, D_MODEL), lambda bi, j: (bi, blk_of(j), 0))
    vt_blk = pl.BlockSpec((1, D_MODEL, SCAN_BLOCK), lambda bi, j: (bi * nblk + blk_of(j), 0, 0))
    in_specs = [blk, blk, blk, blk, vt_blk]
    scratch = [pltpu.VMEM((REC_HEADS, REC_V_DIM, REC_EXPAND), F32)]
    args = (q, k, lf, v, vt)
    if reverse:
        in_specs += [blk, blk, _const_spec((1, REC_V_DIM)), blk, _const_spec((D_MODEL, D_MODEL))]
        scratch += [pltpu.VMEM((SCAN_BLOCK, D_MODEL), BF16)]
        args += tuple(tail)
    return pl.pallas_call(
        functools.partial(_rec_scan_kernel, reverse=reverse),
        grid=(b, nblk),
        in_specs=in_specs,
        out_specs=blk,
        out_shape=jax.ShapeDtypeStruct((b, s, D_MODEL), F32 if reverse else BF16),
        scratch_shapes=scratch,
        compiler_params=pltpu.CompilerParams(dimension_semantics=("parallel", "arbitrary"),
                                             vmem_limit_bytes=VMEM_LIMIT),
        name="rec_scan_bwd_out" if reverse else "rec_scan_fwd",
    )(*args)


def _trunk(x, p):
    b, s, d = x.shape
    t = b * s
    x = x.reshape(t, d)
    for layer in range(DEPTH):
        j = layer // N_MIXERS
        h, hn = _ffn(x, p["ffn1_norm"][layer], p["ffn1_w_gate"][layer], p["ffn1_w_up"][layer],
                     p["ffn1_w_down"][layer], p["mix_norm"][layer], final=False)
        if layer % N_MIXERS == 0:
            q, k, v = _attn_qkv(hn, p["attn_w_in"][j], p["attn_gq"][j], p["attn_gk"][j])
            h = _attn_core(q, k, v.reshape(b, s, KV_DIM), p["attn_sink"][j], h.reshape(b, s, d),
                           p["attn_w_out"][j]).reshape(t, d)
        else:
            q, lf_fw, lf_bw, k_fw, k_bw, iv, ivt, g = _rec_proj(hn, p["rec_w_in"][j], p["rec_w_it"][j], p["lb"][j])
            r3 = lambda a: a.reshape(b, s, d)
            o_fw = _rec_scan(r3(q), r3(k_fw), r3(lf_fw), r3(iv), ivt, reverse=False)
            tail = (o_fw, r3(g), p["rec_o_gain"][j], r3(h), p["rec_w_out"][j])
            h = _rec_scan(r3(q), r3(k_bw), r3(lf_bw), r3(iv), ivt, tail, reverse=True).reshape(t, d)
        x = _ffn(h, p["ffn2_norm"][layer], p["ffn2_w_gate"][layer], p["ffn2_w_up"][layer],
                 p["ffn2_w_down"][layer], p["out_norm"][layer], final=True)
    return x.reshape(b, s, d)


def kernel(x_prompt, x_sample, ffn1_norm, ffn1_w_gate, ffn1_w_up, ffn1_w_down, mix_norm, attn_w_in, attn_q_gain,
           attn_k_gain, attn_sink, attn_w_out, rec_w_in, rec_lb_logits, rec_o_gain, rec_w_out, ffn2_norm,
           ffn2_w_gate, ffn2_w_up, ffn2_w_down, out_norm):
    lb = jnp.cumsum(jax.nn.softmax(rec_lb_logits.astype(F32), axis=0), axis=0)
    lb = lb - lb[0:1]
    row = lambda a: a.astype(F32)[:, None, :]
    p = {
        "ffn1_norm": row(ffn1_norm), "mix_norm": row(mix_norm), "ffn2_norm": row(ffn2_norm),
        "out_norm": row(out_norm),
        "ffn1_w_gate": ffn1_w_gate.astype(BF16), "ffn1_w_up": ffn1_w_up.astype(BF16),
        "ffn1_w_down": ffn1_w_down.astype(BF16),
        "ffn2_w_gate": ffn2_w_gate.astype(BF16), "ffn2_w_up": ffn2_w_up.astype(BF16),
        "ffn2_w_down": ffn2_w_down.astype(BF16),
        "attn_w_in": attn_w_in.astype(BF16), "attn_w_out": attn_w_out.astype(BF16),
        "attn_gq": jnp.tile(attn_q_gain.astype(F32), (1, N_Q_HEADS))[:, None, :] * (HEAD_DIM ** -0.5 * LOG2E),
        "attn_gk": jnp.tile(attn_k_gain.astype(F32), (1, N_KV_HEADS))[:, None, :],
        "attn_sink": attn_sink.astype(F32),
        "rec_w_in": rec_w_in.astype(BF16), "rec_w_out": rec_w_out.astype(BF16),
        "rec_w_it": jnp.swapaxes(rec_w_in[:, :, 3 * D_MODEL:4 * D_MODEL], 1, 2).astype(BF16),
        "rec_o_gain": row(rec_o_gain), "lb": lb,
    }
    return (_trunk(x_prompt, p), _trunk(x_sample, p))
```

```python
import functools

import numpy as np
import jax
import jax.numpy as jnp
from jax import lax
from jax.experimental import pallas as pl
from jax.experimental.pallas import tpu as pltpu

D_MODEL = 1024
DEPTH = 4
N_MIXERS = 2
HEAD_DIM = 64
N_Q_HEADS = D_MODEL // HEAD_DIM
N_KV_HEADS = N_Q_HEADS // 4
GQA_GROUP = N_Q_HEADS // N_KV_HEADS
KV_DIM = N_KV_HEADS * HEAD_DIM
WINDOW = 128
ATTN_BLOCK = 128
REC_EXPAND = 128
REC_HEADS = D_MODEL // REC_EXPAND
REC_V_DIM = D_MODEL // REC_HEADS
D_FF = 2816
EPS = 1e-6

F32 = jnp.float32
BF16 = jnp.bfloat16

FFN_TM = 512
FFN_CHUNK = 256
QKV_TM = 1024
REC_PROJ_TM = 512
REC_PROJ_CHUNK = 256
SCAN_CHUNK = 128
SCAN_BLOCK = 256
VMEM_LIMIT = 56 * 1024 * 1024

_NT = (((1,), (1,)), ((), ()))
_TN = (((0,), (0,)), ((), ()))


def _const_spec(shape):
    return pl.BlockSpec(shape, lambda *_: (0,) * len(shape), pipeline_mode=pl.Buffered(1))


def _rms_scale(x):
    return lax.rsqrt(jnp.mean(x * x, axis=-1, keepdims=True) + EPS)


def _ffn_kernel(x_ref, g1_ref, wg_ref, wu_ref, wd_ref, g2_ref, *rest, final):
    if final:
        y_ref, acc_ref = rest
    else:
        h_ref, hn_ref, acc_ref = rest
    x = x_ref[...]
    n = (x * _rms_scale(x) * g1_ref[...]).astype(BF16)
    for c in range(D_FF // FFN_CHUNK):
        sl = slice(c * FFN_CHUNK, (c + 1) * FFN_CHUNK)
        g = jnp.dot(n, wg_ref[:, sl], preferred_element_type=F32)
        u = jnp.dot(n, wu_ref[:, sl], preferred_element_type=F32)
        a = (g * jax.nn.sigmoid(g) * u).astype(BF16)
        d = jnp.dot(a, wd_ref[sl, :], preferred_element_type=F32)
        if c == 0:
            acc_ref[...] = d
        else:
            acc_ref[...] += d
    h = x + 0.5 * acc_ref[...]
    hn = h * _rms_scale(h) * g2_ref[...]
    if final:
        y_ref[...] = hn
    else:
        h_ref[...] = h
        hn_ref[...] = hn.astype(BF16)


def _ffn(x, g1, wg, wu, wd, g2, *, final):
    t = x.shape[0]
    tm = min(FFN_TM, t)
    row = pl.BlockSpec((tm, D_MODEL), lambda i: (i, 0))
    if final:
        out_shape = jax.ShapeDtypeStruct((t, D_MODEL), F32)
        out_specs = row
    else:
        out_shape = (jax.ShapeDtypeStruct((t, D_MODEL), F32), jax.ShapeDtypeStruct((t, D_MODEL), BF16))
        out_specs = (row, row)
    return pl.pallas_call(
        functools.partial(_ffn_kernel, final=final),
        grid=(t // tm,),
        in_specs=[row, _const_spec((1, D_MODEL)), _const_spec((D_MODEL, D_FF)), _const_spec((D_MODEL, D_FF)),
                  _const_spec((D_FF, D_MODEL)), _const_spec((1, D_MODEL))],
        out_specs=out_specs,
        out_shape=out_shape,
        scratch_shapes=[pltpu.VMEM((tm, D_MODEL), F32)],
        compiler_params=pltpu.CompilerParams(dimension_semantics=("parallel",), vmem_limit_bytes=VMEM_LIMIT),
        name="ffn_final" if final else "ffn_mid",
    )(x, g1, wg, wu, wd, g2)


def _group_mean_matrix(n, group):
    idx = np.arange(n) // group
    return jnp.asarray((idx[:, None] == idx[None, :]).astype(np.float32) / group, dtype=BF16)


def _qkv_kernel(hn_ref, w_ref, gq_ref, gk_ref, gm_ref, q_ref, k_ref, v_ref):
    hn = hn_ref[...]
    gm = gm_ref[...]
    qd = N_Q_HEADS * HEAD_DIM

    def normed(cols, gain):
        x = jnp.dot(hn, w_ref[:, cols], preferred_element_type=F32)
        ms = jnp.dot((x * x).astype(BF16), gm, preferred_element_type=F32)
        return (x * lax.rsqrt(ms + EPS) * gain).astype(BF16)

    for c in range(qd // KV_DIM):
        sl = slice(c * KV_DIM, (c + 1) * KV_DIM)
        q = normed(sl, gq_ref[:, sl])
        for j in range(N_KV_HEADS):
            for rb in range(q_ref.shape[0]):
                q_ref[rb, c * N_KV_HEADS + j] = q[rb * ATTN_BLOCK:(rb + 1) * ATTN_BLOCK,
                                                  j * HEAD_DIM:(j + 1) * HEAD_DIM]
    k = normed(slice(qd, qd + KV_DIM), gk_ref[...])
    for j in range(N_KV_HEADS):
        for rb in range(k_ref.shape[0]):
            k_ref[rb, j] = k[rb * ATTN_BLOCK:(rb + 1) * ATTN_BLOCK, j * HEAD_DIM:(j + 1) * HEAD_DIM]
    v_ref[...] = jnp.dot(hn, w_ref[:, qd + KV_DIM:], preferred_element_type=F32).astype(BF16)


def _attn_qkv(hn, w_in, gq, gk):
    t = hn.shape[0]
    tm = min(QKV_TM, t)
    nrb = tm // ATTN_BLOCK
    in_dim = w_in.shape[1]
    return pl.pallas_call(
        _qkv_kernel,
        grid=(t // tm,),
        in_specs=[pl.BlockSpec((tm, D_MODEL), lambda i: (i, 0)), _const_spec((D_MODEL, in_dim)),
                  _const_spec((1, D_MODEL)), _const_spec((1, KV_DIM)), _const_spec((KV_DIM, KV_DIM))],
        out_specs=(pl.BlockSpec((nrb, N_Q_HEADS, ATTN_BLOCK, HEAD_DIM), lambda i: (i, 0, 0, 0)),
                   pl.BlockSpec((nrb, N_KV_HEADS, ATTN_BLOCK, HEAD_DIM), lambda i: (i, 0, 0, 0)),
                   pl.BlockSpec((tm, KV_DIM), lambda i: (i, 0))),
        out_shape=(jax.ShapeDtypeStruct((t // ATTN_BLOCK, N_Q_HEADS, ATTN_BLOCK, HEAD_DIM), BF16),
                   jax.ShapeDtypeStruct((t // ATTN_BLOCK, N_KV_HEADS, ATTN_BLOCK, HEAD_DIM), BF16),
                   jax.ShapeDtypeStruct((t, KV_DIM), BF16)),
        compiler_params=pltpu.CompilerParams(dimension_semantics=("parallel",), vmem_limit_bytes=VMEM_LIMIT),
        name="attn_qkv",
    )(hn, w_in, gq, gk, _group_mean_matrix(KV_DIM, HEAD_DIM))


LOG2E = float(np.log2(np.e))
_ALIBI_SLOPES = [float(2.0 ** (-8.0 * (i + 1) / N_Q_HEADS)) for i in range(N_Q_HEADS)]
_GROUP_ORDER = (0, 2, 1, 3)
_MASKED = -1e30


def _per_head_rows(values, rows_per_head, shape):
    row = lax.broadcasted_iota(jnp.int32, shape, 0)
    out = jnp.full(shape, values[-1], F32)
    for j in range(len(values) - 2, -1, -1):
        out = jnp.where(row < (j + 1) * rows_per_head, values[j], out)
    return out


def _attn_core_kernel(sink_ref, q_ref, kp_ref, kc_ref, kn_ref, vp_ref, vc_ref, vn_ref, h_ref, wo_ref,
                      out_ref, att_ref, bias_ref, *, n_blocks):
    n = pl.program_id(1)
    blk = ATTN_BLOCK
    kb = 3 * blk
    gm = GQA_GROUP * blk

    @pl.when(n == 0)
    def _():
        row = lax.broadcasted_iota(jnp.int32, (gm, kb), 0)
        col = lax.broadcasted_iota(jnp.int32, (gm, kb), 1)
        absd = jnp.abs(col - WINDOW - (row & (blk - 1)))
        absd_f = absd.astype(F32)
        for g in range(N_KV_HEADS):
            slopes = [_ALIBI_SLOPES[g * GQA_GROUP + j] * LOG2E for j in _GROUP_ORDER]
            bias_ref[g] = jnp.where(absd <= WINDOW, -_per_head_rows(slopes, blk, (gm, kb)) * absd_f, _MASKED)

    col1 = lax.broadcasted_iota(jnp.int32, (1, kb), 1)
    edge = jnp.where(col1 < blk, jnp.where(n == 0, _MASKED, 0.0),
                     jnp.where(col1 >= 2 * blk, jnp.where(n == n_blocks - 1, _MASKED, 0.0), 0.0))
    lane = lax.broadcasted_iota(jnp.int32, (blk, 2 * HEAD_DIM), 1)
    ones = jnp.ones((kb, HEAD_DIM), BF16)
    for g in range(N_KV_HEADS):
        heads = [g * GQA_GROUP + j for j in _GROUP_ORDER]
        qg = jnp.concatenate([q_ref[0, hd] for hd in heads], axis=0)
        kg = jnp.concatenate([kp_ref[0, g], kc_ref[0, g], kn_ref[0, g]], axis=0)
        ks = slice(g * HEAD_DIM, (g + 1) * HEAD_DIM)
        vg = jnp.concatenate([vp_ref[0, :, ks], vc_ref[0, :, ks], vn_ref[0, :, ks]], axis=0)
        s = lax.dot_general(qg, kg, _NT, preferred_element_type=F32) + bias_ref[g] + edge
        sink = _per_head_rows([sink_ref[hd] * LOG2E for hd in heads], blk, (gm, 1))
        m = jnp.maximum(jnp.max(s, axis=-1, keepdims=True), sink)
        p = jnp.exp2(s - m).astype(BF16)
        sink_term = jnp.exp2(sink - m)
        half = gm // 2
        o_even = jnp.dot(p[:half], jnp.concatenate([vg, ones], axis=1), preferred_element_type=F32)
        o_odd = jnp.dot(p[half:], jnp.concatenate([ones, vg], axis=1), preferred_element_type=F32)
        a_even = o_even / (pltpu.roll(o_even, HEAD_DIM, 1) + sink_term[:half])
        a_odd = o_odd / (pltpu.roll(o_odd, HEAD_DIM, 1) + sink_term[half:])
        for pair in range(GQA_GROUP // 2):
            rows = slice(pair * blk, (pair + 1) * blk)
            cols = slice((g * GQA_GROUP + 2 * pair) * HEAD_DIM, (g * GQA_GROUP + 2 * pair + 2) * HEAD_DIM)
            att_ref[:, cols] = jnp.where(lane < HEAD_DIM, a_even[rows], a_odd[rows]).astype(BF16)
    out_ref[0] = h_ref[0] + jnp.dot(att_ref[...], wo_ref[...], preferred_element_type=F32)


def _attn_core(q, k, v, sink, h, w_out):
    b, s, _ = h.shape
    nb = s // ATTN_BLOCK
    cur = lambda bi, n: (bi, n, 0)
    hcur = lambda bi, n: (bi * nb + n, 0, 0, 0)
    hprev = lambda bi, n: (bi * nb + jnp.maximum(n - 1, 0), 0, 0, 0)
    hnext = lambda bi, n: (bi * nb + jnp.minimum(n + 1, nb - 1), 0, 0, 0)
    prev = lambda bi, n: (bi, jnp.maximum(n - 1, 0), 0)
    nxt = lambda bi, n: (bi, jnp.minimum(n + 1, nb - 1), 0)
    q_blk = (1, N_Q_HEADS, ATTN_BLOCK, HEAD_DIM)
    k_blk = (1, N_KV_HEADS, ATTN_BLOCK, HEAD_DIM)
    v_blk = (1, ATTN_BLOCK, KV_DIM)
    d_blk = (1, ATTN_BLOCK, D_MODEL)
    return pl.pallas_call(
        functools.partial(_attn_core_kernel, n_blocks=nb),
        grid=(b, nb),
        in_specs=[pl.BlockSpec(memory_space=pltpu.SMEM),
                  pl.BlockSpec(q_blk, hcur),
                  pl.BlockSpec(k_blk, hprev), pl.BlockSpec(k_blk, hcur), pl.BlockSpec(k_blk, hnext),
                  pl.BlockSpec(v_blk, prev), pl.BlockSpec(v_blk, cur), pl.BlockSpec(v_blk, nxt),
                  pl.BlockSpec(d_blk, cur), _const_spec((D_MODEL, D_MODEL))],
        out_specs=pl.BlockSpec(d_blk, cur),
        out_shape=jax.ShapeDtypeStruct((b, s, D_MODEL), F32),
        scratch_shapes=[pltpu.VMEM((ATTN_BLOCK, D_MODEL), BF16),
                        pltpu.VMEM((N_KV_HEADS, GQA_GROUP * ATTN_BLOCK, 3 * ATTN_BLOCK), F32)],
        compiler_params=pltpu.CompilerParams(dimension_semantics=("arbitrary", "arbitrary"),
                                             vmem_limit_bytes=VMEM_LIMIT),
        name="attn_core",
    )(sink, q, k, k, k, v, v, v, h, w_out)


def _rec_proj_kernel(hn_ref, w_ref, lb_ref, q_ref, lf_fw_ref, lf_bw_ref, k_fw_ref, k_bw_ref, i_ref, g_ref):
    hn = hn_ref[...]
    f = D_MODEL
    cw = REC_PROJ_CHUNK

    def proj(col):
        return jnp.dot(hn, w_ref[:, col:col + cw], preferred_element_type=F32)

    for c in range(f // cw):
        sl = slice(c * cw, (c + 1) * cw)
        q_ref[:, sl] = proj(c * cw)
        for d, (lf_ref, k_ref) in enumerate(((lf_fw_ref, k_fw_ref), (lf_bw_ref, k_bw_ref))):
            z2 = proj((1 + d) * f + c * cw) * LOG2E
            lb = lb_ref[d:d + 1, sl]
            log_lb = jnp.log2(lb)
            log_1m_lb = jnp.log1p(-lb) * LOG2E
            e = jnp.exp2(-jnp.abs(z2))
            ope = 1.0 + e
            cl = log_1m_lb + (jnp.minimum(z2, 0.0) - jnp.log2(ope))
            lf_ref[:, sl] = jnp.maximum(log_lb, cl) + jnp.log2(1.0 + jnp.exp2(-jnp.abs(log_lb - cl)))
            k_ref[:, sl] = (1.0 - lb) * (jnp.where(z2 >= 0.0, e, 1.0) / ope)
        i_ref[:, sl] = proj(3 * f + c * cw).astype(BF16)
        g_ref[:, sl] = proj(4 * f + c * cw).astype(BF16)


def _rec_proj(hn, w_in, lb):
    t = hn.shape[0]
    tm = min(REC_PROJ_TM, t)
    row = pl.BlockSpec((tm, D_MODEL), lambda i: (i, 0))
    f32o = jax.ShapeDtypeStruct((t, D_MODEL), F32)
    bf16o = jax.ShapeDtypeStruct((t, D_MODEL), BF16)
    return pl.pallas_call(
        _rec_proj_kernel,
        grid=(t // tm,),
        in_specs=[row, _const_spec((D_MODEL, w_in.shape[1])), _const_spec((2, D_MODEL))],
        out_specs=(row,) * 7,
        out_shape=(f32o, f32o, f32o, f32o, f32o, bf16o, bf16o),
        compiler_params=pltpu.CompilerParams(dimension_semantics=("parallel",), vmem_limit_bytes=VMEM_LIMIT),
        name="rec_proj",
    )(hn, w_in, lb)


def _scan_chunk(q, k, lf, v, st, *, reverse):
    c, dk = q.shape
    sub = 8
    nv = c // sub
    r = lax.broadcasted_iota(jnp.int32, (nv, sub, dk), 1)
    pos = (sub - 1 - r) if reverse else r

    def later(x, s):
        return pltpu.roll(x, (sub - s) if reverse else s, 1)

    def bit(m):
        return (pos & m) != 0

    p = lf.reshape(nv, sub, dk)
    for s in (1, 2, 4):
        p = p + jnp.where(pos >= s, later(p, s), 0.0)
    g1 = jnp.where(bit(1), later(p, 1), p)
    h1 = jnp.where(bit(1), p, later(p, sub - 1))
    g2 = jnp.where(bit(2), later(h1, 2), h1)
    h2 = jnp.where(bit(2), h1, later(h1, sub - 2))
    h2_shift = later(h2, 4)
    g4 = jnp.where(bit(4), h2_shift, h2)
    tot8 = jnp.where(bit(4), h2, h2_shift)
    order = range(nv - 1, -1, -1) if reverse else range(nv)
    run = [None] * nv
    b = [None] * nv
    prev = None
    for a in order:
        b[a] = p[a] if prev is None else p[a] + prev
        run[a] = tot8[a] if prev is None else tot8[a] + prev
        prev = run[a]
    tot = prev
    b = jnp.stack(b)
    run = jnp.stack(run)
    q3 = q.reshape(nv, sub, dk)
    k3 = k.reshape(nv, sub, dk)

    rowi = lax.broadcasted_iota(jnp.int32, (c, c), 0)
    coli = lax.broadcasted_iota(jnp.int32, (c, c), 1)
    att = jnp.where(rowi == coli, jnp.sum(q * k, axis=-1, keepdims=True), 0.0)

    def add_level(att, m, x):
        gram = lax.dot_general(x, x, _NT, preferred_element_type=F32)
        r_odd = (rowi & m) != 0
        c_odd = (coli & m) != 0
        same = (rowi & ~(2 * m - 1)) == (coli & ~(2 * m - 1))
        pair = ((~r_odd & c_odd) if reverse else (r_odd & ~c_odd)) & same
        return jnp.where(pair, gram, att)

    for m, g in ((1, g1), (2, g2), (4, g4)):
        z = -jnp.abs(p - g)
        x = (jnp.where(bit(m), q3, k3) * jnp.exp2(z)).astype(BF16).reshape(c, dk)
        att = add_level(att, m, x)
    mb = 1
    while mb < nv:
        shp = (nv // (2 * mb), 2, mb, sub, dk)
        first, second = (1, 0) if reverse else (0, 1)
        b5, q5, k5, run5 = (t.reshape(shp) for t in (b, q3, k3, run))
        g = run5[:, first, (0 if reverse else mb - 1)][:, None]
        x_first = k5[:, first] * jnp.exp2(g - b5[:, first])
        x_second = q5[:, second] * jnp.exp2(b5[:, second] - g)
        halves = [x_second, x_first] if reverse else [x_first, x_second]
        x = jnp.stack(halves, axis=1).astype(BF16).reshape(c, dk)
        att = add_level(att, sub * mb, x)
        mb *= 2
    b2 = b.reshape(c, dk)
    q_in = (q * jnp.exp2(b2)).astype(BF16)
    k_out = (k * jnp.exp2(tot[0:1, :] - b2)).astype(BF16)
    o = lax.dot_general(q_in, st.astype(BF16), _NT, preferred_element_type=F32)
    o = o + jnp.dot(att.astype(BF16), v, preferred_element_type=F32)
    st_new = st * jnp.exp2(tot[0:1, :]) + lax.dot_general(v, k_out, _TN, preferred_element_type=F32)
    return o, st_new


def _rec_scan_kernel(*refs, reverse):
    if reverse:
        q_ref, k_ref, lf_ref, v_ref, of_ref, g_ref, gain_ref, h_ref, wo_ref, out_ref, st_ref, y_ref = refs
    else:
        q_ref, k_ref, lf_ref, v_ref, of_ref, st_ref = refs

    @pl.when(pl.program_id(1) == 0)
    def _():
        st_ref[...] = jnp.zeros_like(st_ref)

    n_chunks = SCAN_BLOCK // SCAN_CHUNK
    for cc in (reversed(range(n_chunks)) if reverse else range(n_chunks)):
        rows = slice(cc * SCAN_CHUNK, (cc + 1) * SCAN_CHUNK)
        for hd in range(REC_HEADS):
            cols = slice(hd * REC_EXPAND, (hd + 1) * REC_EXPAND)
            o, st_new = _scan_chunk(q_ref[0, rows, cols], k_ref[0, rows, cols], lf_ref[0, rows, cols],
                                    v_ref[0, rows, cols], st_ref[hd], reverse=reverse)
            st_ref[hd] = st_new
            if reverse:
                o = o + of_ref[0, rows, cols].astype(F32)
                g = g_ref[0, rows, cols].astype(F32)
                y = o * _rms_scale(o) * gain_ref[...] * (g * jax.nn.sigmoid(g))
                y_ref[rows, cols] = y.astype(BF16)
            else:
                of_ref[0, rows, cols] = o.astype(BF16)
    if reverse:
        out_ref[0] = h_ref[0] + jnp.dot(y_ref[...], wo_ref[...], preferred_element_type=F32)


def _rec_scan(q, k, lf, v, tail=None, *, reverse):
    b, s, _ = q.shape
    nblk = s // SCAN_BLOCK
    blk_of = (lambda j: nblk - 1 - j) if reverse else (lambda j: j)
    blk = pl.BlockSpec((1, SCAN_BLOCK, D_MODEL), lambda bi, j: (bi, blk_of(j), 0))
    in_specs = [blk, blk, blk, blk]
    scratch = [pltpu.VMEM((REC_HEADS, REC_V_DIM, REC_EXPAND), F32)]
    args = (q, k, lf, v)
    if reverse:
        in_specs += [blk, blk, _const_spec((1, REC_V_DIM)), blk, _const_spec((D_MODEL, D_MODEL))]
        scratch += [pltpu.VMEM((SCAN_BLOCK, D_MODEL), BF16)]
        args += tuple(tail)
    return pl.pallas_call(
        functools.partial(_rec_scan_kernel, reverse=reverse),
        grid=(b, nblk),
        in_specs=in_specs,
        out_specs=blk,
        out_shape=jax.ShapeDtypeStruct((b, s, D_MODEL), F32 if reverse else BF16),
        scratch_shapes=scratch,
        compiler_params=pltpu.CompilerParams(dimension_semantics=("parallel", "arbitrary"),
                                             vmem_limit_bytes=VMEM_LIMIT),
        name="rec_scan_bwd_out" if reverse else "rec_scan_fwd",
    )(*args)


def _trunk(x, p):
    b, s, d = x.shape
    t = b * s
    x = x.reshape(t, d)
    for layer in range(DEPTH):
        j = layer // N_MIXERS
        h, hn = _ffn(x, p["ffn1_norm"][layer], p["ffn1_w_gate"][layer], p["ffn1_w_up"][layer],
                     p["ffn1_w_down"][layer], p["mix_norm"][layer], final=False)
        if layer % N_MIXERS == 0:
            q, k, v = _attn_qkv(hn, p["attn_w_in"][j], p["attn_gq"][j], p["attn_gk"][j])
            h = _attn_core(q, k, v.reshape(b, s, KV_DIM), p["attn_sink"][j], h.reshape(b, s, d),
                           p["attn_w_out"][j]).reshape(t, d)
        else:
            q, lf_fw, lf_bw, k_fw, k_bw, iv, g = _rec_proj(hn, p["rec_w_in"][j], p["lb"][j])
            r3 = lambda a: a.reshape(b, s, d)
            o_fw = _rec_scan(r3(q), r3(k_fw), r3(lf_fw), r3(iv), reverse=False)
            tail = (o_fw, r3(g), p["rec_o_gain"][j], r3(h), p["rec_w_out"][j])
            h = _rec_scan(r3(q), r3(k_bw), r3(lf_bw), r3(iv), tail, reverse=True).reshape(t, d)
        x = _ffn(h, p["ffn2_norm"][layer], p["ffn2_w_gate"][layer], p["ffn2_w_up"][layer],
                 p["ffn2_w_down"][layer], p["out_norm"][layer], final=True)
    return x.reshape(b, s, d)


def kernel(x_prompt, x_sample, ffn1_norm, ffn1_w_gate, ffn1_w_up, ffn1_w_down, mix_norm, attn_w_in, attn_q_gain,
           attn_k_gain, attn_sink, attn_w_out, rec_w_in, rec_lb_logits, rec_o_gain, rec_w_out, ffn2_norm,
           ffn2_w_gate, ffn2_w_up, ffn2_w_down, out_norm):
    lb = jnp.cumsum(jax.nn.softmax(rec_lb_logits.astype(F32), axis=0), axis=0)
    lb = lb - lb[0:1]
    row = lambda a: a.astype(F32)[:, None, :]
    p = {
        "ffn1_norm": row(ffn1_norm), "mix_norm": row(mix_norm), "ffn2_norm": row(ffn2_norm),
        "out_norm": row(out_norm),
        "ffn1_w_gate": ffn1_w_gate.astype(BF16), "ffn1_w_up": ffn1_w_up.astype(BF16),
        "ffn1_w_down": ffn1_w_down.astype(BF16),
        "ffn2_w_gate": ffn2_w_gate.astype(BF16), "ffn2_w_up": ffn2_w_up.astype(BF16),
        "ffn2_w_down": ffn2_w_down.astype(BF16),
        "attn_w_in": attn_w_in.astype(BF16), "attn_w_out": attn_w_out.astype(BF16),
        "attn_gq": jnp.tile(attn_q_gain.astype(F32), (1, N_Q_HEADS))[:, None, :] * (HEAD_DIM ** -0.5 * LOG2E),
        "attn_gk": jnp.tile(attn_k_gain.astype(F32), (1, N_KV_HEADS))[:, None, :],
        "attn_sink": attn_sink.astype(F32),
        "rec_w_in": rec_w_in.astype(BF16), "rec_w_out": rec_w_out.astype(BF16),
        "rec_o_gain": row(rec_o_gain), "lb": lb,
    }
    return (_trunk(x_prompt, p), _trunk(x_sample, p))
```

```python
import functools

import numpy as np
import jax
import jax.numpy as jnp
from jax import lax
from jax.experimental import pallas as pl
from jax.experimental.pallas import tpu as pltpu

D_MODEL = 1024
DEPTH = 4
N_MIXERS = 2
HEAD_DIM = 64
N_Q_HEADS = D_MODEL // HEAD_DIM
N_KV_HEADS = N_Q_HEADS // 4
GQA_GROUP = N_Q_HEADS // N_KV_HEADS
KV_DIM = N_KV_HEADS * HEAD_DIM
WINDOW = 128
ATTN_BLOCK = 128
ATTN_STEP_BLOCKS = 4
REC_EXPAND = 128
REC_HEADS = D_MODEL // REC_EXPAND
REC_V_DIM = D_MODEL // REC_HEADS
D_FF = 2816
EPS = 1e-6

F32 = jnp.float32
BF16 = jnp.bfloat16

FFN_TM = 512
FFN_CHUNK = 256
QKV_TM = 1024
REC_PROJ_TM = 512
REC_PROJ_CHUNK = 256
SCAN_CHUNK = 128
SCAN_BLOCK = 512
VMEM_LIMIT = 56 * 1024 * 1024

_NT = (((1,), (1,)), ((), ()))
_TN = (((0,), (0,)), ((), ()))


def _const_spec(shape):
    return pl.BlockSpec(shape, lambda *_: (0,) * len(shape), pipeline_mode=pl.Buffered(1))


def _rms_scale(x):
    return lax.rsqrt(jnp.mean(x * x, axis=-1, keepdims=True) + EPS)


def _ffn_kernel(x_ref, g1_ref, wg_ref, wu_ref, wd_ref, g2_ref, *rest, final):
    if final:
        y_ref, acc_ref = rest
    else:
        h_ref, hn_ref, acc_ref = rest
    x = x_ref[...]
    n = (x * _rms_scale(x) * g1_ref[...]).astype(BF16)
    for c in range(D_FF // FFN_CHUNK):
        sl = slice(c * FFN_CHUNK, (c + 1) * FFN_CHUNK)
        g = jnp.dot(n, wg_ref[:, sl], preferred_element_type=F32)
        u = jnp.dot(n, wu_ref[:, sl], preferred_element_type=F32)
        a = (g * jax.nn.sigmoid(g) * u).astype(BF16)
        d = jnp.dot(a, wd_ref[sl, :], preferred_element_type=F32)
        if c == 0:
            acc_ref[...] = d
        else:
            acc_ref[...] += d
    h = x + 0.5 * acc_ref[...]
    hn = h * _rms_scale(h) * g2_ref[...]
    if final:
        y_ref[...] = hn
    else:
        h_ref[...] = h
        hn_ref[...] = hn.astype(BF16)


def _ffn(x, g1, wg, wu, wd, g2, *, final):
    t = x.shape[0]
    tm = min(FFN_TM, t)
    row = pl.BlockSpec((tm, D_MODEL), lambda i: (i, 0))
    if final:
        out_shape = jax.ShapeDtypeStruct((t, D_MODEL), F32)
        out_specs = row
    else:
        out_shape = (jax.ShapeDtypeStruct((t, D_MODEL), F32), jax.ShapeDtypeStruct((t, D_MODEL), BF16))
        out_specs = (row, row)
    return pl.pallas_call(
        functools.partial(_ffn_kernel, final=final),
        grid=(t // tm,),
        in_specs=[row, _const_spec((1, D_MODEL)), _const_spec((D_MODEL, D_FF)), _const_spec((D_MODEL, D_FF)),
                  _const_spec((D_FF, D_MODEL)), _const_spec((1, D_MODEL))],
        out_specs=out_specs,
        out_shape=out_shape,
        scratch_shapes=[pltpu.VMEM((tm, D_MODEL), F32)],
        compiler_params=pltpu.CompilerParams(dimension_semantics=("parallel",), vmem_limit_bytes=VMEM_LIMIT),
        name="ffn_final" if final else "ffn_mid",
    )(x, g1, wg, wu, wd, g2)


def _group_mean_matrix(n, group):
    idx = np.arange(n) // group
    return jnp.asarray((idx[:, None] == idx[None, :]).astype(np.float32) / group, dtype=BF16)


def _qkv_kernel(hn_ref, w_ref, gq_ref, gk_ref, gm_ref, q_ref, k_ref, v_ref):
    hn = hn_ref[...]
    gm = gm_ref[...]
    qd = N_Q_HEADS * HEAD_DIM

    def normed(cols, gain):
        x = jnp.dot(hn, w_ref[:, cols], preferred_element_type=F32)
        ms = jnp.dot((x * x).astype(BF16), gm, preferred_element_type=F32)
        return (x * lax.rsqrt(ms + EPS) * gain).astype(BF16)

    for c in range(qd // KV_DIM):
        sl = slice(c * KV_DIM, (c + 1) * KV_DIM)
        q = normed(sl, gq_ref[:, sl])
        for j in range(N_KV_HEADS):
            for rb in range(q_ref.shape[0]):
                q_ref[rb, c * N_KV_HEADS + j] = q[rb * ATTN_BLOCK:(rb + 1) * ATTN_BLOCK,
                                                  j * HEAD_DIM:(j + 1) * HEAD_DIM]
    k = normed(slice(qd, qd + KV_DIM), gk_ref[...])
    for j in range(N_KV_HEADS):
        for rb in range(k_ref.shape[0]):
            k_ref[rb, j] = k[rb * ATTN_BLOCK:(rb + 1) * ATTN_BLOCK, j * HEAD_DIM:(j + 1) * HEAD_DIM]
    v_ref[...] = jnp.dot(hn, w_ref[:, qd + KV_DIM:], preferred_element_type=F32).astype(BF16)


def _attn_qkv(hn, w_in, gq, gk):
    t = hn.shape[0]
    tm = min(QKV_TM, t)
    nrb = tm // ATTN_BLOCK
    in_dim = w_in.shape[1]
    return pl.pallas_call(
        _qkv_kernel,
        grid=(t // tm,),
        in_specs=[pl.BlockSpec((tm, D_MODEL), lambda i: (i, 0)), _const_spec((D_MODEL, in_dim)),
                  _const_spec((1, D_MODEL)), _const_spec((1, KV_DIM)), _const_spec((KV_DIM, KV_DIM))],
        out_specs=(pl.BlockSpec((nrb, N_Q_HEADS, ATTN_BLOCK, HEAD_DIM), lambda i: (i, 0, 0, 0)),
                   pl.BlockSpec((nrb, N_KV_HEADS, ATTN_BLOCK, HEAD_DIM), lambda i: (i, 0, 0, 0)),
                   pl.BlockSpec((tm, KV_DIM), lambda i: (i, 0))),
        out_shape=(jax.ShapeDtypeStruct((t // ATTN_BLOCK, N_Q_HEADS, ATTN_BLOCK, HEAD_DIM), BF16),
                   jax.ShapeDtypeStruct((t // ATTN_BLOCK, N_KV_HEADS, ATTN_BLOCK, HEAD_DIM), BF16),
                   jax.ShapeDtypeStruct((t, KV_DIM), BF16)),
        compiler_params=pltpu.CompilerParams(dimension_semantics=("parallel",), vmem_limit_bytes=VMEM_LIMIT),
        name="attn_qkv",
    )(hn, w_in, gq, gk, _group_mean_matrix(KV_DIM, HEAD_DIM))


LOG2E = float(np.log2(np.e))
_ALIBI_SLOPES = [float(2.0 ** (-8.0 * (i + 1) / N_Q_HEADS)) for i in range(N_Q_HEADS)]
_GROUP_ORDER = (0, 2, 1, 3)
_MASKED = -1e30


def _per_head_rows(values, rows_per_head, shape):
    row = lax.broadcasted_iota(jnp.int32, shape, 0)
    out = jnp.full(shape, values[-1], F32)
    for j in range(len(values) - 2, -1, -1):
        out = jnp.where(row < (j + 1) * rows_per_head, values[j], out)
    return out


def _attn_core_kernel(sink_ref, q_ref, kp_ref, kc_ref, kn_ref, vp_ref, vc_ref, vn_ref, h_ref, wo_ref,
                      out_ref, att_ref, bias_ref, *, n_steps):
    n = pl.program_id(1)
    blk = ATTN_BLOCK
    nq = ATTN_STEP_BLOCKS
    kb = 3 * blk
    gm = GQA_GROUP * blk

    @pl.when(n == 0)
    def _():
        row = lax.broadcasted_iota(jnp.int32, (gm, kb), 0)
        col = lax.broadcasted_iota(jnp.int32, (gm, kb), 1)
        absd = jnp.abs(col - WINDOW - (row & (blk - 1)))
        absd_f = absd.astype(F32)
        for g in range(N_KV_HEADS):
            slopes = [_ALIBI_SLOPES[g * GQA_GROUP + j] * LOG2E for j in _GROUP_ORDER]
            bias_ref[g] = jnp.where(absd <= WINDOW, -_per_head_rows(slopes, blk, (gm, kb)) * absd_f, _MASKED)

    def k_block(j, g):
        return kp_ref[0, g] if j < 0 else kn_ref[0, g] if j == nq else kc_ref[j, g]

    def v_block(j, cols):
        if j < 0:
            return vp_ref[0, :, cols]
        return vn_ref[0, :, cols] if j == nq else vc_ref[0, j * blk:(j + 1) * blk, cols]

    col1 = lax.broadcasted_iota(jnp.int32, (1, kb), 1)
    lane = lax.broadcasted_iota(jnp.int32, (blk, 2 * HEAD_DIM), 1)
    ones = jnp.ones((kb, HEAD_DIM), BF16)
    for i in range(nq):
        edge = None
        if i == 0:
            edge = jnp.where((col1 < blk) & (n == 0), _MASKED, 0.0)
        if i == nq - 1:
            last = jnp.where((col1 >= 2 * blk) & (n == n_steps - 1), _MASKED, 0.0)
            edge = last if edge is None else edge + last
        for g in range(N_KV_HEADS):
            heads = [g * GQA_GROUP + j for j in _GROUP_ORDER]
            qg = jnp.concatenate([q_ref[i, hd] for hd in heads], axis=0)
            kg = jnp.concatenate([k_block(j, g) for j in (i - 1, i, i + 1)], axis=0)
            ks = slice(g * HEAD_DIM, (g + 1) * HEAD_DIM)
            vg = jnp.concatenate([v_block(j, ks) for j in (i - 1, i, i + 1)], axis=0)
            s = lax.dot_general(qg, kg, _NT, preferred_element_type=F32) + bias_ref[g]
            if edge is not None:
                s = s + edge
            sink = _per_head_rows([sink_ref[hd] * LOG2E for hd in heads], blk, (gm, 1))
            m = jnp.maximum(jnp.max(s, axis=-1, keepdims=True), sink)
            p = jnp.exp2(s - m).astype(BF16)
            sink_term = jnp.exp2(sink - m)
            half = gm // 2
            o_even = jnp.dot(p[:half], jnp.concatenate([vg, ones], axis=1), preferred_element_type=F32)
            o_odd = jnp.dot(p[half:], jnp.concatenate([ones, vg], axis=1), preferred_element_type=F32)
            a_even = o_even / (pltpu.roll(o_even, HEAD_DIM, 1) + sink_term[:half])
            a_odd = o_odd / (pltpu.roll(o_odd, HEAD_DIM, 1) + sink_term[half:])
            for pair in range(GQA_GROUP // 2):
                rows = slice(pair * blk, (pair + 1) * blk)
                cols = slice((g * GQA_GROUP + 2 * pair) * HEAD_DIM, (g * GQA_GROUP + 2 * pair + 2) * HEAD_DIM)
                att_ref[i * blk:(i + 1) * blk, cols] = jnp.where(lane < HEAD_DIM, a_even[rows],
                                                                 a_odd[rows]).astype(BF16)
    out_ref[0] = h_ref[0] + jnp.dot(att_ref[...], wo_ref[...], preferred_element_type=F32)


def _attn_core(q, k, v, sink, h, w_out):
    b, s, _ = h.shape
    nq = ATTN_STEP_BLOCKS
    nb = s // ATTN_BLOCK
    ns = nb // nq
    prev_blk = lambda n: jnp.maximum(n * nq - 1, 0)
    next_blk = lambda n: jnp.minimum(n * nq + nq, nb - 1)
    head_major = lambda heads, rows_of: pl.BlockSpec(
        (rows_of[0], heads, ATTN_BLOCK, HEAD_DIM), lambda bi, n: (rows_of[1](bi, n), 0, 0, 0))
    mid = (nq, lambda bi, n: bi * ns + n)
    before = (1, lambda bi, n: bi * nb + prev_blk(n))
    after = (1, lambda bi, n: bi * nb + next_blk(n))
    v_edge = lambda f: pl.BlockSpec((1, ATTN_BLOCK, KV_DIM), lambda bi, n: (bi, f(n), 0))
    step_rows = lambda width: pl.BlockSpec((1, nq * ATTN_BLOCK, width), lambda bi, n: (bi, n, 0))
    return pl.pallas_call(
        functools.partial(_attn_core_kernel, n_steps=ns),
        grid=(b, ns),
        in_specs=[pl.BlockSpec(memory_space=pltpu.SMEM),
                  head_major(N_Q_HEADS, mid),
                  head_major(N_KV_HEADS, before), head_major(N_KV_HEADS, mid), head_major(N_KV_HEADS, after),
                  v_edge(prev_blk), step_rows(KV_DIM), v_edge(next_blk),
                  step_rows(D_MODEL), _const_spec((D_MODEL, D_MODEL))],
        out_specs=step_rows(D_MODEL),
        out_shape=jax.ShapeDtypeStruct((b, s, D_MODEL), F32),
        scratch_shapes=[pltpu.VMEM((nq * ATTN_BLOCK, D_MODEL), BF16),
                        pltpu.VMEM((N_KV_HEADS, GQA_GROUP * ATTN_BLOCK, 3 * ATTN_BLOCK), F32)],
        compiler_params=pltpu.CompilerParams(dimension_semantics=("arbitrary", "arbitrary"),
                                             vmem_limit_bytes=VMEM_LIMIT),
        name="attn_core",
    )(sink, q, k, k, k, v, v, v, h, w_out)


def _forget_gate(z2, lb):
    log_lb = jnp.log2(lb)
    log_1m_lb = jnp.log1p(-lb) * LOG2E
    e = jnp.exp2(-jnp.abs(z2))
    ope = 1.0 + e
    cl = log_1m_lb + (jnp.minimum(z2, 0.0) - jnp.log2(ope))
    log_f = jnp.maximum(log_lb, cl) + jnp.log2(1.0 + jnp.exp2(-jnp.abs(log_lb - cl)))
    return log_f, (1.0 - lb) * (jnp.where(z2 >= 0.0, e, 1.0) / ope)


def _rec_proj_kernel(hn_ref, w_ref, lb_ref, q_ref, lf_fw_ref, lf_bw_ref, k_fw_ref, k_bw_ref, i_ref, g_ref):
    hn = hn_ref[...]
    f = D_MODEL
    cw = REC_PROJ_CHUNK

    def proj(col):
        return jnp.dot(hn, w_ref[:, col:col + cw], preferred_element_type=F32)

    for c in range(f // cw):
        sl = slice(c * cw, (c + 1) * cw)
        q_ref[:, sl] = proj(c * cw)
        for d, (lf_ref, k_ref) in enumerate(((lf_fw_ref, k_fw_ref), (lf_bw_ref, k_bw_ref))):
            lf_ref[:, sl], k_ref[:, sl] = _forget_gate(proj((1 + d) * f + c * cw) * LOG2E, lb_ref[d:d + 1, sl])
        i_ref[:, sl] = proj(3 * f + c * cw).astype(BF16)
        g_ref[:, sl] = proj(4 * f + c * cw).astype(BF16)


def _rec_proj(hn, w_in, lb):
    t = hn.shape[0]
    tm = min(REC_PROJ_TM, t)
    row = pl.BlockSpec((tm, D_MODEL), lambda i: (i, 0))
    f32o = jax.ShapeDtypeStruct((t, D_MODEL), F32)
    bf16o = jax.ShapeDtypeStruct((t, D_MODEL), BF16)
    return pl.pallas_call(
        _rec_proj_kernel,
        grid=(t // tm,),
        in_specs=[row, _const_spec((D_MODEL, w_in.shape[1])), _const_spec((2, D_MODEL))],
        out_specs=(row,) * 7,
        out_shape=(f32o, f32o, f32o, f32o, f32o, bf16o, bf16o),
        compiler_params=pltpu.CompilerParams(dimension_semantics=("parallel",), vmem_limit_bytes=VMEM_LIMIT),
        name="rec_proj",
    )(hn, w_in, lb)


def _scan_chunk(q, k, lf, v, st, *, reverse):
    c, dk = q.shape
    sub = 8
    nv = c // sub
    r = lax.broadcasted_iota(jnp.int32, (nv, sub, dk), 1)
    pos = (sub - 1 - r) if reverse else r

    def later(x, s):
        return pltpu.roll(x, (sub - s) if reverse else s, 1)

    def bit(m):
        return (pos & m) != 0

    p = lf.reshape(nv, sub, dk)
    for s in (1, 2, 4):
        p = p + jnp.where(pos >= s, later(p, s), 0.0)
    g1 = jnp.where(bit(1), later(p, 1), p)
    h1 = jnp.where(bit(1), p, later(p, sub - 1))
    g2 = jnp.where(bit(2), later(h1, 2), h1)
    h2 = jnp.where(bit(2), h1, later(h1, sub - 2))
    h2_shift = later(h2, 4)
    g4 = jnp.where(bit(4), h2_shift, h2)
    tot8 = jnp.where(bit(4), h2, h2_shift)
    order = range(nv - 1, -1, -1) if reverse else range(nv)
    run = [None] * nv
    b = [None] * nv
    prev = None
    for a in order:
        b[a] = p[a] if prev is None else p[a] + prev
        run[a] = tot8[a] if prev is None else tot8[a] + prev
        prev = run[a]
    tot = prev
    b = jnp.stack(b)
    run = jnp.stack(run)
    q3 = q.reshape(nv, sub, dk)
    k3 = k.reshape(nv, sub, dk)

    rowi = lax.broadcasted_iota(jnp.int32, (c, c), 0)
    coli = lax.broadcasted_iota(jnp.int32, (c, c), 1)
    att = jnp.where(rowi == coli, jnp.sum(q * k, axis=-1, keepdims=True), 0.0)

    def add_level(att, m, x):
        gram = lax.dot_general(x, x, _NT, preferred_element_type=F32)
        r_odd = (rowi & m) != 0
        c_odd = (coli & m) != 0
        same = (rowi & ~(2 * m - 1)) == (coli & ~(2 * m - 1))
        pair = ((~r_odd & c_odd) if reverse else (r_odd & ~c_odd)) & same
        return jnp.where(pair, gram, att)

    for m, g in ((1, g1), (2, g2), (4, g4)):
        z = -jnp.abs(p - g)
        x = (jnp.where(bit(m), q3, k3) * jnp.exp2(z)).astype(BF16).reshape(c, dk)
        att = add_level(att, m, x)
    mb = 1
    while mb < nv:
        shp = (nv // (2 * mb), 2, mb, sub, dk)
        first, second = (1, 0) if reverse else (0, 1)
        b5, q5, k5, run5 = (t.reshape(shp) for t in (b, q3, k3, run))
        g = run5[:, first, (0 if reverse else mb - 1)][:, None]
        x_first = k5[:, first] * jnp.exp2(g - b5[:, first])
        x_second = q5[:, second] * jnp.exp2(b5[:, second] - g)
        halves = [x_second, x_first] if reverse else [x_first, x_second]
        x = jnp.stack(halves, axis=1).astype(BF16).reshape(c, dk)
        att = add_level(att, sub * mb, x)
        mb *= 2
    b2 = b.reshape(c, dk)
    q_in = (q * jnp.exp2(b2)).astype(BF16)
    k_out = (k * jnp.exp2(tot[0:1, :] - b2)).astype(BF16)
    o = lax.dot_general(q_in, st.astype(BF16), _NT, preferred_element_type=F32)
    o = o + jnp.dot(att.astype(BF16), v, preferred_element_type=F32)
    st_new = st * jnp.exp2(tot[0:1, :]) + lax.dot_general(v, k_out, _TN, preferred_element_type=F32)
    return o, st_new


def _rec_scan_kernel(*refs, reverse):
    if reverse:
        q_ref, k_ref, lf_ref, v_ref, of_ref, g_ref, gain_ref, h_ref, wo_ref, out_ref, st_ref, y_ref = refs
    else:
        q_ref, k_ref, lf_ref, v_ref, of_ref, st_ref = refs

    @pl.when(pl.program_id(1) == 0)
    def _():
        st_ref[...] = jnp.zeros_like(st_ref)

    n_chunks = SCAN_BLOCK // SCAN_CHUNK
    for cc in (reversed(range(n_chunks)) if reverse else range(n_chunks)):
        rows = slice(cc * SCAN_CHUNK, (cc + 1) * SCAN_CHUNK)
        for hd in range(REC_HEADS):
            cols = slice(hd * REC_EXPAND, (hd + 1) * REC_EXPAND)
            o, st_new = _scan_chunk(q_ref[0, rows, cols], k_ref[0, rows, cols], lf_ref[0, rows, cols],
                                    v_ref[0, rows, cols], st_ref[hd], reverse=reverse)
            st_ref[hd] = st_new
            if reverse:
                o = o + of_ref[0, rows, cols].astype(F32)
                g = g_ref[0, rows, cols].astype(F32)
                y = o * _rms_scale(o) * gain_ref[...] * (g * jax.nn.sigmoid(g))
                y_ref[rows, cols] = y.astype(BF16)
            else:
                of_ref[0, rows, cols] = o.astype(BF16)
    if reverse:
        out_ref[0] = h_ref[0] + jnp.dot(y_ref[...], wo_ref[...], preferred_element_type=F32)


def _rec_scan(q, k, lf, v, tail=None, *, reverse):
    b, s, _ = q.shape
    nblk = s // SCAN_BLOCK
    blk_of = (lambda j: nblk - 1 - j) if reverse else (lambda j: j)
    blk = pl.BlockSpec((1, SCAN_BLOCK, D_MODEL), lambda bi, j: (bi, blk_of(j), 0))
    in_specs = [blk, blk, blk, blk]
    scratch = [pltpu.VMEM((REC_HEADS, REC_V_DIM, REC_EXPAND), F32)]
    args = (q, k, lf, v)
    if reverse:
        in_specs += [blk, blk, _const_spec((1, REC_V_DIM)), blk, _const_spec((D_MODEL, D_MODEL))]
        scratch += [pltpu.VMEM((SCAN_BLOCK, D_MODEL), BF16)]
        args += tuple(tail)
    return pl.pallas_call(
        functools.partial(_rec_scan_kernel, reverse=reverse),
        grid=(b, nblk),
        in_specs=in_specs,
        out_specs=blk,
        out_shape=jax.ShapeDtypeStruct((b, s, D_MODEL), F32 if reverse else BF16),
        scratch_shapes=scratch,
        compiler_params=pltpu.CompilerParams(dimension_semantics=("parallel", "arbitrary"),
                                             vmem_limit_bytes=VMEM_LIMIT),
        name="rec_scan_bwd_out" if reverse else "rec_scan_fwd",
    )(*args)


def _trunk(x, p):
    b, s, d = x.shape
    t = b * s
    x = x.reshape(t, d)
    for layer in range(DEPTH):
        j = layer // N_MIXERS
        h, hn = _ffn(x, p["ffn1_norm"][layer], p["ffn1_w_gate"][layer], p["ffn1_w_up"][layer],
                     p["ffn1_w_down"][layer], p["mix_norm"][layer], final=False)
        if layer % N_MIXERS == 0:
            q, k, v = _attn_qkv(hn, p["attn_w_in"][j], p["attn_gq"][j], p["attn_gk"][j])
            h = _attn_core(q, k, v.reshape(b, s, KV_DIM), p["attn_sink"][j], h.reshape(b, s, d),
                           p["attn_w_out"][j]).reshape(t, d)
        else:
            q, lf_fw, lf_bw, k_fw, k_bw, iv, g = _rec_proj(hn, p["rec_w_in"][j], p["lb"][j])
            r3 = lambda a: a.reshape(b, s, d)
            o_fw = _rec_scan(r3(q), r3(k_fw), r3(lf_fw), r3(iv), reverse=False)
            tail = (o_fw, r3(g), p["rec_o_gain"][j], r3(h), p["rec_w_out"][j])
            h = _rec_scan(r3(q), r3(k_bw), r3(lf_bw), r3(iv), tail, reverse=True).reshape(t, d)
        x = _ffn(h, p["ffn2_norm"][layer], p["ffn2_w_gate"][layer], p["ffn2_w_up"][layer],
                 p["ffn2_w_down"][layer], p["out_norm"][layer], final=True)
    return x.reshape(b, s, d)


def kernel(x_prompt, x_sample, ffn1_norm, ffn1_w_gate, ffn1_w_up, ffn1_w_down, mix_norm, attn_w_in, attn_q_gain,
           attn_k_gain, attn_sink, attn_w_out, rec_w_in, rec_lb_logits, rec_o_gain, rec_w_out, ffn2_norm,
           ffn2_w_gate, ffn2_w_up, ffn2_w_down, out_norm):
    lb = jnp.cumsum(jax.nn.softmax(rec_lb_logits.astype(F32), axis=0), axis=0)
    lb = lb - lb[0:1]
    row = lambda a: a.astype(F32)[:, None, :]
    p = {
        "ffn1_norm": row(ffn1_norm), "mix_norm": row(mix_norm), "ffn2_norm": row(ffn2_norm),
        "out_norm": row(out_norm),
        "ffn1_w_gate": ffn1_w_gate.astype(BF16), "ffn1_w_up": ffn1_w_up.astype(BF16),
        "ffn1_w_down": ffn1_w_down.astype(BF16),
        "ffn2_w_gate": ffn2_w_gate.astype(BF16), "ffn2_w_up": ffn2_w_up.astype(BF16),
        "ffn2_w_down": ffn2_w_down.astype(BF16),
        "attn_w_in": attn_w_in.astype(BF16), "attn_w_out": attn_w_out.astype(BF16),
        "attn_gq": jnp.tile(attn_q_gain.astype(F32), (1, N_Q_HEADS))[:, None, :] * (HEAD_DIM ** -0.5 * LOG2E),
        "attn_gk": jnp.tile(attn_k_gain.astype(F32), (1, N_KV_HEADS))[:, None, :],
        "attn_sink": attn_sink.astype(F32),
        "rec_w_in": rec_w_in.astype(BF16), "rec_w_out": rec_w_out.astype(BF16),
        "rec_o_gain": row(rec_o_gain), "lb": lb,
    }
    return (_trunk(x_prompt, p), _trunk(x_sample, p))
```

```python
import functools

import numpy as np
import jax
import jax.numpy as jnp
from jax import lax
from jax.experimental import pallas as pl
from jax.experimental.pallas import tpu as pltpu

D_MODEL = 1024
DEPTH = 4
N_MIXERS = 2
HEAD_DIM = 64
N_Q_HEADS = D_MODEL // HEAD_DIM
N_KV_HEADS = N_Q_HEADS // 4
GQA_GROUP = N_Q_HEADS // N_KV_HEADS
KV_DIM = N_KV_HEADS * HEAD_DIM
WINDOW = 128
ATTN_BLOCK = 128
ATTN_STEP_BLOCKS = 4
ATTN_LOOKAHEAD = 1
REC_EXPAND = 128
REC_HEADS = D_MODEL // REC_EXPAND
REC_V_DIM = D_MODEL // REC_HEADS
D_FF = 2816
EPS = 1e-6

F32 = jnp.float32
BF16 = jnp.bfloat16

FFN_TM = 1024
FFN_CHUNK = 256
QKV_TM = 1024
REC_PROJ_TM = 512
REC_PROJ_CHUNK = 256
SCAN_CHUNK = 128
SCAN_BLOCK = 512
VMEM_LIMIT = 56 * 1024 * 1024

_NT = (((1,), (1,)), ((), ()))
_TN = (((0,), (0,)), ((), ()))


def _const_spec(shape):
    return pl.BlockSpec(shape, lambda *_: (0,) * len(shape), pipeline_mode=pl.Buffered(1))


def _rms_scale(x):
    return lax.rsqrt(jnp.mean(x * x, axis=-1, keepdims=True) + EPS)


def _ffn_kernel(x_ref, g1_ref, wg_ref, wu_ref, wd_ref, g2_ref, *rest, final):
    if final:
        y_ref, acc_ref = rest
    else:
        h_ref, hn_ref, acc_ref = rest
    x = x_ref[...]
    n = (x * _rms_scale(x) * g1_ref[...]).astype(BF16)
    for c in range(D_FF // FFN_CHUNK):
        sl = slice(c * FFN_CHUNK, (c + 1) * FFN_CHUNK)
        g = jnp.dot(n, wg_ref[:, sl], preferred_element_type=F32)
        u = jnp.dot(n, wu_ref[:, sl], preferred_element_type=F32)
        a = (g * jax.nn.sigmoid(g) * u).astype(BF16)
        d = jnp.dot(a, wd_ref[sl, :], preferred_element_type=F32)
        if c == 0:
            acc_ref[...] = d
        else:
            acc_ref[...] += d
    h = x + 0.5 * acc_ref[...]
    hn = h * _rms_scale(h) * g2_ref[...]
    if final:
        y_ref[...] = hn
    else:
        h_ref[...] = h
        hn_ref[...] = hn.astype(BF16)


def _ffn(x, g1, wg, wu, wd, g2, *, final):
    t = x.shape[0]
    tm = min(FFN_TM, t)
    row = pl.BlockSpec((tm, D_MODEL), lambda i: (i, 0))
    if final:
        out_shape = jax.ShapeDtypeStruct((t, D_MODEL), F32)
        out_specs = row
    else:
        out_shape = (jax.ShapeDtypeStruct((t, D_MODEL), F32), jax.ShapeDtypeStruct((t, D_MODEL), BF16))
        out_specs = (row, row)
    return pl.pallas_call(
        functools.partial(_ffn_kernel, final=final),
        grid=(t // tm,),
        in_specs=[row, _const_spec((1, D_MODEL)), _const_spec((D_MODEL, D_FF)), _const_spec((D_MODEL, D_FF)),
                  _const_spec((D_FF, D_MODEL)), _const_spec((1, D_MODEL))],
        out_specs=out_specs,
        out_shape=out_shape,
        scratch_shapes=[pltpu.VMEM((tm, D_MODEL), F32)],
        compiler_params=pltpu.CompilerParams(dimension_semantics=("parallel",), vmem_limit_bytes=VMEM_LIMIT),
        name="ffn_final" if final else "ffn_mid",
    )(x, g1, wg, wu, wd, g2)


def _group_mean_matrix(n, group):
    idx = np.arange(n) // group
    return jnp.asarray((idx[:, None] == idx[None, :]).astype(np.float32) / group, dtype=BF16)


def _qkv_kernel(hn_ref, w_ref, gq_ref, gk_ref, gm_ref, q_ref, k_ref, v_ref):
    hn = hn_ref[...]
    gm = gm_ref[...]
    qd = N_Q_HEADS * HEAD_DIM

    def proj(c):
        return jnp.dot(hn, w_ref[:, c * KV_DIM:(c + 1) * KV_DIM], preferred_element_type=F32)

    def store_normed(c, x):
        is_q = c < qd // KV_DIM
        gain = gq_ref[:, c * KV_DIM:(c + 1) * KV_DIM] if is_q else gk_ref[...]
        ms = jnp.dot((x * x).astype(BF16), gm, preferred_element_type=F32)
        y = (x * lax.rsqrt(ms + EPS) * gain).astype(BF16)
        out_ref, head0 = (q_ref, c * N_KV_HEADS) if is_q else (k_ref, 0)
        for j in range(N_KV_HEADS):
            for rb in range(out_ref.shape[0]):
                out_ref[rb, head0 + j] = y[rb * ATTN_BLOCK:(rb + 1) * ATTN_BLOCK, j * HEAD_DIM:(j + 1) * HEAD_DIM]

    n_norm = (qd + KV_DIM) // KV_DIM
    x_next = proj(0)
    for c in range(n_norm):
        x = x_next
        x_next = proj(c + 1)
        if c + 1 == n_norm:
            v_ref[...] = x_next.astype(BF16)
        store_normed(c, x)


def _attn_qkv(hn, w_in, gq, gk):
    t = hn.shape[0]
    tm = min(QKV_TM, t)
    nrb = tm // ATTN_BLOCK
    in_dim = w_in.shape[1]
    return pl.pallas_call(
        _qkv_kernel,
        grid=(t // tm,),
        in_specs=[pl.BlockSpec((tm, D_MODEL), lambda i: (i, 0)), _const_spec((D_MODEL, in_dim)),
                  _const_spec((1, D_MODEL)), _const_spec((1, KV_DIM)), _const_spec((KV_DIM, KV_DIM))],
        out_specs=(pl.BlockSpec((nrb, N_Q_HEADS, ATTN_BLOCK, HEAD_DIM), lambda i: (i, 0, 0, 0)),
                   pl.BlockSpec((nrb, N_KV_HEADS, ATTN_BLOCK, HEAD_DIM), lambda i: (i, 0, 0, 0)),
                   pl.BlockSpec((tm, KV_DIM), lambda i: (i, 0))),
        out_shape=(jax.ShapeDtypeStruct((t // ATTN_BLOCK, N_Q_HEADS, ATTN_BLOCK, HEAD_DIM), BF16),
                   jax.ShapeDtypeStruct((t // ATTN_BLOCK, N_KV_HEADS, ATTN_BLOCK, HEAD_DIM), BF16),
                   jax.ShapeDtypeStruct((t, KV_DIM), BF16)),
        compiler_params=pltpu.CompilerParams(dimension_semantics=("parallel",), vmem_limit_bytes=VMEM_LIMIT),
        name="attn_qkv",
    )(hn, w_in, gq, gk, _group_mean_matrix(KV_DIM, HEAD_DIM))


LOG2E = float(np.log2(np.e))
_ALIBI_SLOPES = [float(2.0 ** (-8.0 * (i + 1) / N_Q_HEADS)) for i in range(N_Q_HEADS)]
_GROUP_ORDER = (0, 2, 1, 3)
_MASKED = -1e30


def _per_head_rows(values, rows_per_head, shape):
    row = lax.broadcasted_iota(jnp.int32, shape, 0)
    out = jnp.full(shape, values[-1], F32)
    for j in range(len(values) - 2, -1, -1):
        out = jnp.where(row < (j + 1) * rows_per_head, values[j], out)
    return out


def _attn_core_kernel(sink_ref, q_ref, kp_ref, kc_ref, kn_ref, vp_ref, vc_ref, vn_ref, h_ref, wo_ref,
                      out_ref, att_ref, bias_ref, *, n_steps):
    n = pl.program_id(1)
    blk = ATTN_BLOCK
    nq = ATTN_STEP_BLOCKS
    kb = 3 * blk
    gm = GQA_GROUP * blk

    @pl.when(n == 0)
    def _():
        row = lax.broadcasted_iota(jnp.int32, (gm, kb), 0)
        col = lax.broadcasted_iota(jnp.int32, (gm, kb), 1)
        absd = jnp.abs(col - WINDOW - (row & (blk - 1)))
        absd_f = absd.astype(F32)
        for g in range(N_KV_HEADS):
            slopes = [_ALIBI_SLOPES[g * GQA_GROUP + j] * LOG2E for j in _GROUP_ORDER]
            bias_ref[g] = jnp.where(absd <= WINDOW, -_per_head_rows(slopes, blk, (gm, kb)) * absd_f, _MASKED)

    def k_block(j, g):
        return kp_ref[0, g] if j < 0 else kn_ref[0, g] if j == nq else kc_ref[j, g]

    def v_block(j, cols):
        if j < 0:
            return vp_ref[0, :, cols]
        return vn_ref[0, :, cols] if j == nq else vc_ref[0, j * blk:(j + 1) * blk, cols]

    col1 = lax.broadcasted_iota(jnp.int32, (1, kb), 1)
    lane = lax.broadcasted_iota(jnp.int32, (blk, 2 * HEAD_DIM), 1)
    ones = jnp.ones((kb, HEAD_DIM), BF16)
    def scores(i, g):
        heads = [g * GQA_GROUP + j for j in _GROUP_ORDER]
        qg = jnp.concatenate([q_ref[i, hd] for hd in heads], axis=0)
        kg = jnp.concatenate([k_block(j, g) for j in (i - 1, i, i + 1)], axis=0)
        s = lax.dot_general(qg, kg, _NT, preferred_element_type=F32) + bias_ref[g]
        if i == 0:
            s = s + jnp.where((col1 < blk) & (n == 0), _MASKED, 0.0)
        if i == nq - 1:
            s = s + jnp.where((col1 >= 2 * blk) & (n == n_steps - 1), _MASKED, 0.0)
        return s

    def attend(i, g, s):
        heads = [g * GQA_GROUP + j for j in _GROUP_ORDER]
        ks = slice(g * HEAD_DIM, (g + 1) * HEAD_DIM)
        vg = jnp.concatenate([v_block(j, ks) for j in (i - 1, i, i + 1)], axis=0)
        sink = _per_head_rows([sink_ref[hd] * LOG2E for hd in heads], blk, (gm, 1))
        m = jnp.maximum(jnp.max(s, axis=-1, keepdims=True), sink)
        p = jnp.exp2(s - m).astype(BF16)
        sink_term = jnp.exp2(sink - m)
        half = gm // 2
        o_even = jnp.dot(p[:half], jnp.concatenate([vg, ones], axis=1), preferred_element_type=F32)
        o_odd = jnp.dot(p[half:], jnp.concatenate([ones, vg], axis=1), preferred_element_type=F32)
        a_even = o_even / (pltpu.roll(o_even, HEAD_DIM, 1) + sink_term[:half])
        a_odd = o_odd / (pltpu.roll(o_odd, HEAD_DIM, 1) + sink_term[half:])
        for pair in range(GQA_GROUP // 2):
            rows = slice(pair * blk, (pair + 1) * blk)
            cols = slice((g * GQA_GROUP + 2 * pair) * HEAD_DIM, (g * GQA_GROUP + 2 * pair + 2) * HEAD_DIM)
            att_ref[i * blk:(i + 1) * blk, cols] = jnp.where(lane < HEAD_DIM, a_even[rows],
                                                             a_odd[rows]).astype(BF16)

    pending = []
    for i in range(nq):
        for g in range(N_KV_HEADS):
            pending.append((i, g, scores(i, g)))
            if len(pending) > ATTN_LOOKAHEAD:
                attend(*pending.pop(0))
    for item in pending:
        attend(*item)
    out_ref[0] = h_ref[0] + jnp.dot(att_ref[...], wo_ref[...], preferred_element_type=F32)


def _attn_core(q, k, v, sink, h, w_out):
    b, s, _ = h.shape
    nq = ATTN_STEP_BLOCKS
    nb = s // ATTN_BLOCK
    ns = nb // nq
    prev_blk = lambda n: jnp.maximum(n * nq - 1, 0)
    next_blk = lambda n: jnp.minimum(n * nq + nq, nb - 1)
    head_major = lambda heads, rows_of: pl.BlockSpec(
        (rows_of[0], heads, ATTN_BLOCK, HEAD_DIM), lambda bi, n: (rows_of[1](bi, n), 0, 0, 0))
    mid = (nq, lambda bi, n: bi * ns + n)
    before = (1, lambda bi, n: bi * nb + prev_blk(n))
    after = (1, lambda bi, n: bi * nb + next_blk(n))
    v_edge = lambda f: pl.BlockSpec((1, ATTN_BLOCK, KV_DIM), lambda bi, n: (bi, f(n), 0))
    step_rows = lambda width: pl.BlockSpec((1, nq * ATTN_BLOCK, width), lambda bi, n: (bi, n, 0))
    return pl.pallas_call(
        functools.partial(_attn_core_kernel, n_steps=ns),
        grid=(b, ns),
        in_specs=[pl.BlockSpec(memory_space=pltpu.SMEM),
                  head_major(N_Q_HEADS, mid),
                  head_major(N_KV_HEADS, before), head_major(N_KV_HEADS, mid), head_major(N_KV_HEADS, after),
                  v_edge(prev_blk), step_rows(KV_DIM), v_edge(next_blk),
                  step_rows(D_MODEL), _const_spec((D_MODEL, D_MODEL))],
        out_specs=step_rows(D_MODEL),
        out_shape=jax.ShapeDtypeStruct((b, s, D_MODEL), F32),
        scratch_shapes=[pltpu.VMEM((nq * ATTN_BLOCK, D_MODEL), BF16),
                        pltpu.VMEM((N_KV_HEADS, GQA_GROUP * ATTN_BLOCK, 3 * ATTN_BLOCK), F32)],
        compiler_params=pltpu.CompilerParams(dimension_semantics=("arbitrary", "arbitrary"),
                                             vmem_limit_bytes=VMEM_LIMIT),
        name="attn_core",
    )(sink, q, k, k, k, v, v, v, h, w_out)


def _forget_gate(z2, lb):
    log_lb = jnp.log2(lb)
    log_1m_lb = jnp.log1p(-lb) * LOG2E
    e = jnp.exp2(-jnp.abs(z2))
    ope = 1.0 + e
    cl = log_1m_lb + (jnp.minimum(z2, 0.0) - jnp.log2(ope))
    log_f = jnp.maximum(log_lb, cl) + jnp.log2(1.0 + jnp.exp2(-jnp.abs(log_lb - cl)))
    return log_f, (1.0 - lb) * (jnp.where(z2 >= 0.0, e, 1.0) / ope)


def _rec_proj_kernel(hn_ref, w_ref, lb_ref, q_ref, lf_fw_ref, lf_bw_ref, k_fw_ref, k_bw_ref, i_ref, g_ref):
    hn = hn_ref[...]
    f = D_MODEL
    cw = REC_PROJ_CHUNK

    def proj(col):
        return jnp.dot(hn, w_ref[:, col:col + cw], preferred_element_type=F32)

    for c in range(f // cw):
        sl = slice(c * cw, (c + 1) * cw)
        q, z_fw, z_bw, i, g = (proj(n * f + c * cw) for n in range(5))
        q_ref[:, sl] = q
        i_ref[:, sl] = i.astype(BF16)
        g_ref[:, sl] = g.astype(BF16)
        lf_fw_ref[:, sl], k_fw_ref[:, sl] = _forget_gate(z_fw * LOG2E, lb_ref[0:1, sl])
        lf_bw_ref[:, sl], k_bw_ref[:, sl] = _forget_gate(z_bw * LOG2E, lb_ref[1:2, sl])


def _rec_proj(hn, w_in, lb):
    t = hn.shape[0]
    tm = min(REC_PROJ_TM, t)
    row = pl.BlockSpec((tm, D_MODEL), lambda i: (i, 0))
    f32o = jax.ShapeDtypeStruct((t, D_MODEL), F32)
    bf16o = jax.ShapeDtypeStruct((t, D_MODEL), BF16)
    return pl.pallas_call(
        _rec_proj_kernel,
        grid=(t // tm,),
        in_specs=[row, _const_spec((D_MODEL, w_in.shape[1])), _const_spec((2, D_MODEL))],
        out_specs=(row,) * 7,
        out_shape=(f32o, f32o, f32o, f32o, f32o, bf16o, bf16o),
        compiler_params=pltpu.CompilerParams(dimension_semantics=("parallel",), vmem_limit_bytes=VMEM_LIMIT),
        name="rec_proj",
    )(hn, w_in, lb)


def _neg_abs(x):
    bits = lax.bitcast_convert_type(x, jnp.uint32) | jnp.uint32(0x80000000)
    return lax.bitcast_convert_type(bits, F32)


def _scan_intra(q, k, lf, *, reverse):
    c, dk = q.shape
    sub = 8
    nv = c // sub
    r = lax.broadcasted_iota(jnp.int32, (nv, sub, dk), 1)
    pos = (sub - 1 - r) if reverse else r

    def later(x, s):
        return pltpu.roll(x, (sub - s) if reverse else s, 1)

    def bit(m):
        return (pos & m) != 0

    p = lf.reshape(nv, sub, dk)
    for s in (1, 2, 4):
        p = p + jnp.where(pos >= s, later(p, s), 0.0)
    g1 = jnp.where(bit(1), later(p, 1), p)
    h1 = jnp.where(bit(1), p, later(p, sub - 1))
    g2 = jnp.where(bit(2), later(h1, 2), h1)
    h2 = jnp.where(bit(2), h1, later(h1, sub - 2))
    h2_shift = later(h2, 4)
    g4 = jnp.where(bit(4), h2_shift, h2)
    tot8 = jnp.where(bit(4), h2, h2_shift)
    order = range(nv - 1, -1, -1) if reverse else range(nv)
    run = [None] * nv
    b = [None] * nv
    prev = None
    for a in order:
        b[a] = p[a] if prev is None else p[a] + prev
        run[a] = tot8[a] if prev is None else tot8[a] + prev
        prev = run[a]
    tot = prev
    b = jnp.stack(b)
    run = jnp.stack(run)
    q3 = q.reshape(nv, sub, dk)
    k3 = k.reshape(nv, sub, dk)

    rowi = lax.broadcasted_iota(jnp.int32, (c, c), 0)
    coli = lax.broadcasted_iota(jnp.int32, (c, c), 1)
    att = jnp.where(rowi == coli, jnp.sum(q * k, axis=-1, keepdims=True), 0.0)

    def add_level(att, m, x):
        gram = lax.dot_general(x, x, _NT, preferred_element_type=F32)
        r_odd = (rowi & m) != 0
        c_odd = (coli & m) != 0
        same = (rowi & ~(2 * m - 1)) == (coli & ~(2 * m - 1))
        pair = ((~r_odd & c_odd) if reverse else (r_odd & ~c_odd)) & same
        return jnp.where(pair, gram, att)

    for m, g in ((1, g1), (2, g2), (4, g4)):
        z = _neg_abs(p - g)
        x = (jnp.where(bit(m), q3, k3) * jnp.exp2(z)).astype(BF16).reshape(c, dk)
        att = add_level(att, m, x)
    mb = 1
    while mb < nv:
        shp = (nv // (2 * mb), 2, mb, sub, dk)
        first, second = (1, 0) if reverse else (0, 1)
        b5, q5, k5, run5 = (t.reshape(shp) for t in (b, q3, k3, run))
        g = run5[:, first, (0 if reverse else mb - 1)][:, None]
        x_first = k5[:, first] * jnp.exp2(g - b5[:, first])
        x_second = q5[:, second] * jnp.exp2(b5[:, second] - g)
        halves = [x_second, x_first] if reverse else [x_first, x_second]
        x = jnp.stack(halves, axis=1).astype(BF16).reshape(c, dk)
        att = add_level(att, sub * mb, x)
        mb *= 2
    b2 = b.reshape(c, dk)
    q_in = (q * jnp.exp2(b2)).astype(BF16)
    k_out = (k * jnp.exp2(tot[0:1, :] - b2)).astype(BF16)
    return att.astype(BF16), q_in, k_out, tot[0:1, :]


def _scan_inter(att, q_in, k_out, tot, v, st):
    o = lax.dot_general(q_in, st.astype(BF16), _NT, preferred_element_type=F32)
    o = o + jnp.dot(att, v, preferred_element_type=F32)
    st_new = st * jnp.exp2(tot) + lax.dot_general(v, k_out, _TN, preferred_element_type=F32)
    return o, st_new


def _rec_scan_kernel(*refs, reverse):
    if reverse:
        (q_ref, k_ref, lf_ref, v_ref, of_ref, g_ref, gain_ref, h_ref, wo_ref, out_ref, st_ref, att_ref, qin_ref,
         kout_ref, tot_ref, y_ref) = refs
    else:
        q_ref, k_ref, lf_ref, v_ref, of_ref, st_ref, att_ref, qin_ref, kout_ref, tot_ref = refs

    @pl.when(pl.program_id(1) == 0)
    def _():
        st_ref[...] = jnp.zeros_like(st_ref)

    n_chunks = SCAN_BLOCK // SCAN_CHUNK
    chunk_order = list(reversed(range(n_chunks)) if reverse else range(n_chunks))
    def intra(cc, hd):
        rows = slice(cc * SCAN_CHUNK, (cc + 1) * SCAN_CHUNK)
        cols = slice(hd * REC_EXPAND, (hd + 1) * REC_EXPAND)
        att_ref[cc, hd], qin_ref[rows, cols], kout_ref[rows, cols], tot_ref[cc, :, cols] = _scan_intra(
            q_ref[0, rows, cols], k_ref[0, rows, cols], lf_ref[0, rows, cols], reverse=reverse)

    def inter(cc, hd):
        rows = slice(cc * SCAN_CHUNK, (cc + 1) * SCAN_CHUNK)
        cols = slice(hd * REC_EXPAND, (hd + 1) * REC_EXPAND)
        o, st_new = _scan_inter(att_ref[cc, hd], qin_ref[rows, cols], kout_ref[rows, cols],
                                tot_ref[cc, :, cols], v_ref[0, rows, cols], st_ref[hd])
        st_ref[hd] = st_new
        if reverse:
            o = o + of_ref[0, rows, cols].astype(F32)
            g = g_ref[0, rows, cols].astype(F32)
            y = o * _rms_scale(o) * gain_ref[...] * (g * jax.nn.sigmoid(g))
            y_ref[rows, cols] = y.astype(BF16)
        else:
            of_ref[0, rows, cols] = o.astype(BF16)

    for idx, cc in enumerate(chunk_order):
        for hd in range(REC_HEADS):
            intra(cc, hd)
            if idx > 0:
                inter(chunk_order[idx - 1], hd)
    for hd in range(REC_HEADS):
        inter(chunk_order[-1], hd)
    if reverse:
        out_ref[0] = h_ref[0] + jnp.dot(y_ref[...], wo_ref[...], preferred_element_type=F32)


def _rec_scan(q, k, lf, v, tail=None, *, reverse):
    b, s, _ = q.shape
    nblk = s // SCAN_BLOCK
    blk_of = (lambda j: nblk - 1 - j) if reverse else (lambda j: j)
    blk = pl.BlockSpec((1, SCAN_BLOCK, D_MODEL), lambda bi, j: (bi, blk_of(j), 0))
    in_specs = [blk, blk, blk, blk]
    n_chunks = SCAN_BLOCK // SCAN_CHUNK
    scratch = [pltpu.VMEM((REC_HEADS, REC_V_DIM, REC_EXPAND), F32),
               pltpu.VMEM((n_chunks, REC_HEADS, SCAN_CHUNK, SCAN_CHUNK), BF16),
               pltpu.VMEM((SCAN_BLOCK, D_MODEL), BF16), pltpu.VMEM((SCAN_BLOCK, D_MODEL), BF16),
               pltpu.VMEM((n_chunks, 1, D_MODEL), F32)]
    args = (q, k, lf, v)
    if reverse:
        in_specs += [blk, blk, _const_spec((1, REC_V_DIM)), blk, _const_spec((D_MODEL, D_MODEL))]
        scratch += [pltpu.VMEM((SCAN_BLOCK, D_MODEL), BF16)]
        args += tuple(tail)
    return pl.pallas_call(
        functools.partial(_rec_scan_kernel, reverse=reverse),
        grid=(b, nblk),
        in_specs=in_specs,
        out_specs=blk,
        out_shape=jax.ShapeDtypeStruct((b, s, D_MODEL), F32 if reverse else BF16),
        scratch_shapes=scratch,
        compiler_params=pltpu.CompilerParams(dimension_semantics=("parallel", "arbitrary"),
                                             vmem_limit_bytes=VMEM_LIMIT),
        name="rec_scan_bwd_out" if reverse else "rec_scan_fwd",
    )(*args)


def _trunk(x, p):
    b, s, d = x.shape
    t = b * s
    x = x.reshape(t, d)
    for layer in range(DEPTH):
        j = layer // N_MIXERS
        h, hn = _ffn(x, p["ffn1_norm"][layer], p["ffn1_w_gate"][layer], p["ffn1_w_up"][layer],
                     p["ffn1_w_down"][layer], p["mix_norm"][layer], final=False)
        if layer % N_MIXERS == 0:
            q, k, v = _attn_qkv(hn, p["attn_w_in"][j], p["attn_gq"][j], p["attn_gk"][j])
            h = _attn_core(q, k, v.reshape(b, s, KV_DIM), p["attn_sink"][j], h.reshape(b, s, d),
                           p["attn_w_out"][j]).reshape(t, d)
        else:
            q, lf_fw, lf_bw, k_fw, k_bw, iv, g = _rec_proj(hn, p["rec_w_in"][j], p["lb"][j])
            r3 = lambda a: a.reshape(b, s, d)
            o_fw = _rec_scan(r3(q), r3(k_fw), r3(lf_fw), r3(iv), reverse=False)
            tail = (o_fw, r3(g), p["rec_o_gain"][j], r3(h), p["rec_w_out"][j])
            h = _rec_scan(r3(q), r3(k_bw), r3(lf_bw), r3(iv), tail, reverse=True).reshape(t, d)
        x = _ffn(h, p["ffn2_norm"][layer], p["ffn2_w_gate"][layer], p["ffn2_w_up"][layer],
                 p["ffn2_w_down"][layer], p["out_norm"][layer], final=True)
    return x.reshape(b, s, d)


def kernel(x_prompt, x_sample, ffn1_norm, ffn1_w_gate, ffn1_w_up, ffn1_w_down, mix_norm, attn_w_in, attn_q_gain,
           attn_k_gain, attn_sink, attn_w_out, rec_w_in, rec_lb_logits, rec_o_gain, rec_w_out, ffn2_norm,
           ffn2_w_gate, ffn2_w_up, ffn2_w_down, out_norm):
    lb = jnp.cumsum(jax.nn.softmax(rec_lb_logits.astype(F32), axis=0), axis=0)
    lb = lb - lb[0:1]
    row = lambda a: a.astype(F32)[:, None, :]
    p = {
        "ffn1_norm": row(ffn1_norm), "mix_norm": row(mix_norm), "ffn2_norm": row(ffn2_norm),
        "out_norm": row(out_norm),
        "ffn1_w_gate": ffn1_w_gate.astype(BF16), "ffn1_w_up": ffn1_w_up.astype(BF16),
        "ffn1_w_down": ffn1_w_down.astype(BF16),
        "ffn2_w_gate": ffn2_w_gate.astype(BF16), "ffn2_w_up": ffn2_w_up.astype(BF16),
        "ffn2_w_down": ffn2_w_down.astype(BF16),
        "attn_w_in": attn_w_in.astype(BF16), "attn_w_out": attn_w_out.astype(BF16),
        "attn_gq": jnp.tile(attn_q_gain.astype(F32), (1, N_Q_HEADS))[:, None, :] * (HEAD_DIM ** -0.5 * LOG2E),
        "attn_gk": jnp.tile(attn_k_gain.astype(F32), (1, N_KV_HEADS))[:, None, :],
        "attn_sink": attn_sink.astype(F32),
        "rec_w_in": rec_w_in.astype(BF16), "rec_w_out": rec_w_out.astype(BF16),
        "rec_o_gain": row(rec_o_gain), "lb": lb,
    }
    return (_trunk(x_prompt, p), _trunk(x_sample, p))
```

```python
import functools

import numpy as np
import jax
import jax.numpy as jnp
from jax import lax
from jax.experimental import pallas as pl
from jax.experimental.pallas import tpu as pltpu

D_MODEL = 1024
DEPTH = 4
N_MIXERS = 2
HEAD_DIM = 64
N_Q_HEADS = D_MODEL // HEAD_DIM
N_KV_HEADS = N_Q_HEADS // 4
GQA_GROUP = N_Q_HEADS // N_KV_HEADS
KV_DIM = N_KV_HEADS * HEAD_DIM
WINDOW = 128
ATTN_BLOCK = 128
ATTN_STEP_BLOCKS = 4
ATTN_LOOKAHEAD = 1
REC_EXPAND = 128
REC_HEADS = D_MODEL // REC_EXPAND
REC_V_DIM = D_MODEL // REC_HEADS
D_FF = 2816
EPS = 1e-6

F32 = jnp.float32
BF16 = jnp.bfloat16

FFN_TM = 1024
FFN_ROWS = 512
FFN_CHUNK = 256
QKV_TM = 1024
REC_PROJ_TM = 512
REC_PROJ_CHUNK = 256
SCAN_CHUNK = 128
SCAN_BLOCK = 512
VMEM_LIMIT = 56 * 1024 * 1024

_NT = (((1,), (1,)), ((), ()))
_TN = (((0,), (0,)), ((), ()))


def _const_spec(shape):
    return pl.BlockSpec(shape, lambda *_: (0,) * len(shape), pipeline_mode=pl.Buffered(1))


def _rms_scale(x):
    return lax.rsqrt(jnp.mean(x * x, axis=-1, keepdims=True) + EPS)


def _ffn_kernel(x_ref, g1_ref, wg_ref, wu_ref, wd_ref, g2_ref, *rest, final):
    if final:
        y_ref, acc_ref = rest
    else:
        h_ref, hn_ref, acc_ref = rest
    n_groups = max(1, x_ref.shape[0] // FFN_ROWS)
    for grp in range(n_groups):
        rows = slice(grp * (x_ref.shape[0] // n_groups), (grp + 1) * (x_ref.shape[0] // n_groups))
        x = x_ref[rows]
        n = (x * _rms_scale(x) * g1_ref[...]).astype(BF16)
        for c in range(D_FF // FFN_CHUNK):
            sl = slice(c * FFN_CHUNK, (c + 1) * FFN_CHUNK)
            g = jnp.dot(n, wg_ref[:, sl], preferred_element_type=F32)
            u = jnp.dot(n, wu_ref[:, sl], preferred_element_type=F32)
            a = (g * jax.nn.sigmoid(g) * u).astype(BF16)
            d = jnp.dot(a, wd_ref[sl, :], preferred_element_type=F32)
            if c == 0:
                acc_ref[rows] = d
            else:
                acc_ref[rows] += d
        h = x + 0.5 * acc_ref[rows]
        hn = h * _rms_scale(h) * g2_ref[...]
        if final:
            y_ref[rows] = hn
        else:
            h_ref[rows] = h
            hn_ref[rows] = hn.astype(BF16)


def _ffn(x, g1, wg, wu, wd, g2, *, final):
    t = x.shape[0]
    tm = min(FFN_TM, t)
    row = pl.BlockSpec((tm, D_MODEL), lambda i: (i, 0))
    if final:
        out_shape = jax.ShapeDtypeStruct((t, D_MODEL), F32)
        out_specs = row
    else:
        out_shape = (jax.ShapeDtypeStruct((t, D_MODEL), F32), jax.ShapeDtypeStruct((t, D_MODEL), BF16))
        out_specs = (row, row)
    return pl.pallas_call(
        functools.partial(_ffn_kernel, final=final),
        grid=(t // tm,),
        in_specs=[row, _const_spec((1, D_MODEL)), _const_spec((D_MODEL, D_FF)), _const_spec((D_MODEL, D_FF)),
                  _const_spec((D_FF, D_MODEL)), _const_spec((1, D_MODEL))],
        out_specs=out_specs,
        out_shape=out_shape,
        scratch_shapes=[pltpu.VMEM((tm, D_MODEL), F32)],
        compiler_params=pltpu.CompilerParams(dimension_semantics=("parallel",), vmem_limit_bytes=VMEM_LIMIT),
        name="ffn_final" if final else "ffn_mid",
    )(x, g1, wg, wu, wd, g2)


def _group_mean_matrix(n, group):
    idx = np.arange(n) // group
    return jnp.asarray((idx[:, None] == idx[None, :]).astype(np.float32) / group, dtype=BF16)


def _qkv_kernel(hn_ref, w_ref, gq_ref, gk_ref, gm_ref, q_ref, k_ref, v_ref):
    hn = hn_ref[...]
    gm = gm_ref[...]
    qd = N_Q_HEADS * HEAD_DIM

    def proj(c):
        return jnp.dot(hn, w_ref[:, c * KV_DIM:(c + 1) * KV_DIM], preferred_element_type=F32)

    def store_normed(c, x):
        is_q = c < qd // KV_DIM
        gain = gq_ref[:, c * KV_DIM:(c + 1) * KV_DIM] if is_q else gk_ref[...]
        ms = jnp.dot((x * x).astype(BF16), gm, preferred_element_type=F32)
        y = (x * lax.rsqrt(ms + EPS) * gain).astype(BF16)
        out_ref, head0 = (q_ref, c * N_KV_HEADS) if is_q else (k_ref, 0)
        for j in range(N_KV_HEADS):
            for rb in range(out_ref.shape[0]):
                out_ref[rb, head0 + j] = y[rb * ATTN_BLOCK:(rb + 1) * ATTN_BLOCK, j * HEAD_DIM:(j + 1) * HEAD_DIM]

    n_norm = (qd + KV_DIM) // KV_DIM
    x_next = proj(0)
    for c in range(n_norm):
        x = x_next
        x_next = proj(c + 1)
        if c + 1 == n_norm:
            v_ref[...] = x_next.astype(BF16)
        store_normed(c, x)


def _attn_qkv(hn, w_in, gq, gk):
    t = hn.shape[0]
    tm = min(QKV_TM, t)
    nrb = tm // ATTN_BLOCK
    in_dim = w_in.shape[1]
    return pl.pallas_call(
        _qkv_kernel,
        grid=(t // tm,),
        in_specs=[pl.BlockSpec((tm, D_MODEL), lambda i: (i, 0)), _const_spec((D_MODEL, in_dim)),
                  _const_spec((1, D_MODEL)), _const_spec((1, KV_DIM)), _const_spec((KV_DIM, KV_DIM))],
        out_specs=(pl.BlockSpec((nrb, N_Q_HEADS, ATTN_BLOCK, HEAD_DIM), lambda i: (i, 0, 0, 0)),
                   pl.BlockSpec((nrb, N_KV_HEADS, ATTN_BLOCK, HEAD_DIM), lambda i: (i, 0, 0, 0)),
                   pl.BlockSpec((tm, KV_DIM), lambda i: (i, 0))),
        out_shape=(jax.ShapeDtypeStruct((t // ATTN_BLOCK, N_Q_HEADS, ATTN_BLOCK, HEAD_DIM), BF16),
                   jax.ShapeDtypeStruct((t // ATTN_BLOCK, N_KV_HEADS, ATTN_BLOCK, HEAD_DIM), BF16),
                   jax.ShapeDtypeStruct((t, KV_DIM), BF16)),
        compiler_params=pltpu.CompilerParams(dimension_semantics=("parallel",), vmem_limit_bytes=VMEM_LIMIT),
        name="attn_qkv",
    )(hn, w_in, gq, gk, _group_mean_matrix(KV_DIM, HEAD_DIM))


LOG2E = float(np.log2(np.e))
_ALIBI_SLOPES = [float(2.0 ** (-8.0 * (i + 1) / N_Q_HEADS)) for i in range(N_Q_HEADS)]
_GROUP_ORDER = (0, 2, 1, 3)
_MASKED = -1e30


def _per_head_rows(values, rows_per_head, shape):
    row = lax.broadcasted_iota(jnp.int32, shape, 0)
    out = jnp.full(shape, values[-1], F32)
    for j in range(len(values) - 2, -1, -1):
        out = jnp.where(row < (j + 1) * rows_per_head, values[j], out)
    return out


def _attn_core_kernel(sink_ref, q_ref, kp_ref, kc_ref, kn_ref, vp_ref, vc_ref, vn_ref, h_ref, wo_ref,
                      out_ref, att_ref, bias_ref, *, n_steps):
    n = pl.program_id(1)
    blk = ATTN_BLOCK
    nq = ATTN_STEP_BLOCKS
    kb = 3 * blk
    gm = GQA_GROUP * blk

    @pl.when(n == 0)
    def _():
        row = lax.broadcasted_iota(jnp.int32, (gm, kb), 0)
        col = lax.broadcasted_iota(jnp.int32, (gm, kb), 1)
        absd = jnp.abs(col - WINDOW - (row & (blk - 1)))
        absd_f = absd.astype(F32)
        for g in range(N_KV_HEADS):
            slopes = [_ALIBI_SLOPES[g * GQA_GROUP + j] * LOG2E for j in _GROUP_ORDER]
            bias_ref[g] = jnp.where(absd <= WINDOW, -_per_head_rows(slopes, blk, (gm, kb)) * absd_f, _MASKED)

    def k_block(j, g):
        return kp_ref[0, g] if j < 0 else kn_ref[0, g] if j == nq else kc_ref[j, g]

    def v_block(j, cols):
        if j < 0:
            return vp_ref[0, :, cols]
        return vn_ref[0, :, cols] if j == nq else vc_ref[0, j * blk:(j + 1) * blk, cols]

    col1 = lax.broadcasted_iota(jnp.int32, (1, kb), 1)
    lane = lax.broadcasted_iota(jnp.int32, (blk, 2 * HEAD_DIM), 1)
    ones = jnp.ones((kb, HEAD_DIM), BF16)
    def scores(i, g):
        heads = [g * GQA_GROUP + j for j in _GROUP_ORDER]
        qg = jnp.concatenate([q_ref[i, hd] for hd in heads], axis=0)
        kg = jnp.concatenate([k_block(j, g) for j in (i - 1, i, i + 1)], axis=0)
        s = lax.dot_general(qg, kg, _NT, preferred_element_type=F32) + bias_ref[g]
        if i == 0:
            s = s + jnp.where((col1 < blk) & (n == 0), _MASKED, 0.0)
        if i == nq - 1:
            s = s + jnp.where((col1 >= 2 * blk) & (n == n_steps - 1), _MASKED, 0.0)
        return s

    def attend(i, g, s):
        heads = [g * GQA_GROUP + j for j in _GROUP_ORDER]
        ks = slice(g * HEAD_DIM, (g + 1) * HEAD_DIM)
        vg = jnp.concatenate([v_block(j, ks) for j in (i - 1, i, i + 1)], axis=0)
        sink = _per_head_rows([sink_ref[hd] * LOG2E for hd in heads], blk, (gm, 1))
        m = jnp.maximum(jnp.max(s, axis=-1, keepdims=True), sink)
        p = jnp.exp2(s - m).astype(BF16)
        sink_term = jnp.exp2(sink - m)
        half = gm // 2
        o_even = jnp.dot(p[:half], jnp.concatenate([vg, ones], axis=1), preferred_element_type=F32)
        o_odd = jnp.dot(p[half:], jnp.concatenate([ones, vg], axis=1), preferred_element_type=F32)
        a_even = o_even / (pltpu.roll(o_even, HEAD_DIM, 1) + sink_term[:half])
        a_odd = o_odd / (pltpu.roll(o_odd, HEAD_DIM, 1) + sink_term[half:])
        for pair in range(GQA_GROUP // 2):
            rows = slice(pair * blk, (pair + 1) * blk)
            cols = slice((g * GQA_GROUP + 2 * pair) * HEAD_DIM, (g * GQA_GROUP + 2 * pair + 2) * HEAD_DIM)
            att_ref[i * blk:(i + 1) * blk, cols] = jnp.where(lane < HEAD_DIM, a_even[rows],
                                                             a_odd[rows]).astype(BF16)

    pending = []
    for i in range(nq):
        for g in range(N_KV_HEADS):
            pending.append((i, g, scores(i, g)))
            if len(pending) > ATTN_LOOKAHEAD:
                attend(*pending.pop(0))
    for item in pending:
        attend(*item)
    out_ref[0] = h_ref[0] + jnp.dot(att_ref[...], wo_ref[...], preferred_element_type=F32)


def _attn_core(q, k, v, sink, h, w_out):
    b, s, _ = h.shape
    nq = ATTN_STEP_BLOCKS
    nb = s // ATTN_BLOCK
    ns = nb // nq
    prev_blk = lambda n: jnp.maximum(n * nq - 1, 0)
    next_blk = lambda n: jnp.minimum(n * nq + nq, nb - 1)
    head_major = lambda heads, rows_of: pl.BlockSpec(
        (rows_of[0], heads, ATTN_BLOCK, HEAD_DIM), lambda bi, n: (rows_of[1](bi, n), 0, 0, 0))
    mid = (nq, lambda bi, n: bi * ns + n)
    before = (1, lambda bi, n: bi * nb + prev_blk(n))
    after = (1, lambda bi, n: bi * nb + next_blk(n))
    v_edge = lambda f: pl.BlockSpec((1, ATTN_BLOCK, KV_DIM), lambda bi, n: (bi, f(n), 0))
    step_rows = lambda width: pl.BlockSpec((1, nq * ATTN_BLOCK, width), lambda bi, n: (bi, n, 0))
    return pl.pallas_call(
        functools.partial(_attn_core_kernel, n_steps=ns),
        grid=(b, ns),
        in_specs=[pl.BlockSpec(memory_space=pltpu.SMEM),
                  head_major(N_Q_HEADS, mid),
                  head_major(N_KV_HEADS, before), head_major(N_KV_HEADS, mid), head_major(N_KV_HEADS, after),
                  v_edge(prev_blk), step_rows(KV_DIM), v_edge(next_blk),
                  step_rows(D_MODEL), _const_spec((D_MODEL, D_MODEL))],
        out_specs=step_rows(D_MODEL),
        out_shape=jax.ShapeDtypeStruct((b, s, D_MODEL), F32),
        scratch_shapes=[pltpu.VMEM((nq * ATTN_BLOCK, D_MODEL), BF16),
                        pltpu.VMEM((N_KV_HEADS, GQA_GROUP * ATTN_BLOCK, 3 * ATTN_BLOCK), F32)],
        compiler_params=pltpu.CompilerParams(dimension_semantics=("arbitrary", "arbitrary"),
                                             vmem_limit_bytes=VMEM_LIMIT),
        name="attn_core",
    )(sink, q, k, k, k, v, v, v, h, w_out)


def _forget_gate(z2, lb):
    log_lb = jnp.log2(lb)
    log_1m_lb = jnp.log1p(-lb) * LOG2E
    e = jnp.exp2(-jnp.abs(z2))
    ope = 1.0 + e
    cl = log_1m_lb + (jnp.minimum(z2, 0.0) - jnp.log2(ope))
    log_f = jnp.maximum(log_lb, cl) + jnp.log2(1.0 + jnp.exp2(-jnp.abs(log_lb - cl)))
    return log_f, (1.0 - lb) * (jnp.where(z2 >= 0.0, e, 1.0) / ope)


def _rec_proj_kernel(hn_ref, w_ref, lb_ref, q_ref, lf_fw_ref, lf_bw_ref, k_fw_ref, k_bw_ref, i_ref, g_ref, *z_refs):
    hn = hn_ref[...]
    f = D_MODEL
    cw = REC_PROJ_CHUNK
    n_chunks = f // cw

    def proj(col):
        return jnp.dot(hn, w_ref[:, col:col + cw], preferred_element_type=F32)

    def parked(c, d):
        return z_refs[c % 2].at[lax.rem(pl.program_id(0) + c // 2, 2), d]

    def project(c):
        sl = slice(c * cw, (c + 1) * cw)
        q_ref[:, sl] = proj(c * cw)
        parked(c, 0)[...] = proj(f + c * cw)
        parked(c, 1)[...] = proj(2 * f + c * cw)
        i_ref[:, sl] = proj(3 * f + c * cw).astype(BF16)
        g_ref[:, sl] = proj(4 * f + c * cw).astype(BF16)

    def gates(c):
        sl = slice(c * cw, (c + 1) * cw)
        lf_fw_ref[:, sl], k_fw_ref[:, sl] = _forget_gate(parked(c, 0)[...] * LOG2E, lb_ref[0:1, sl])
        lf_bw_ref[:, sl], k_bw_ref[:, sl] = _forget_gate(parked(c, 1)[...] * LOG2E, lb_ref[1:2, sl])

    project(0)
    for c in range(n_chunks):
        if c + 1 < n_chunks:
            project(c + 1)
        gates(c)


def _rec_proj(hn, w_in, lb):
    t = hn.shape[0]
    tm = min(REC_PROJ_TM, t)
    row = pl.BlockSpec((tm, D_MODEL), lambda i: (i, 0))
    f32o = jax.ShapeDtypeStruct((t, D_MODEL), F32)
    bf16o = jax.ShapeDtypeStruct((t, D_MODEL), BF16)
    return pl.pallas_call(
        _rec_proj_kernel,
        grid=(t // tm,),
        in_specs=[row, _const_spec((D_MODEL, w_in.shape[1])), _const_spec((2, D_MODEL))],
        out_specs=(row,) * 7,
        out_shape=(f32o, f32o, f32o, f32o, f32o, bf16o, bf16o),
        scratch_shapes=[pltpu.VMEM((2, 2, tm, REC_PROJ_CHUNK), F32)] * 2,
        compiler_params=pltpu.CompilerParams(dimension_semantics=("parallel",), vmem_limit_bytes=VMEM_LIMIT),
        name="rec_proj",
    )(hn, w_in, lb)


def _neg_abs(x):
    bits = lax.bitcast_convert_type(x, jnp.uint32) | jnp.uint32(0x80000000)
    return lax.bitcast_convert_type(bits, F32)


def _scan_intra(q, k, lf, *, reverse):
    c, dk = q.shape
    sub = 8
    nv = c // sub
    r = lax.broadcasted_iota(jnp.int32, (nv, sub, dk), 1)
    pos = (sub - 1 - r) if reverse else r

    def later(x, s):
        return pltpu.roll(x, (sub - s) if reverse else s, 1)

    def bit(m):
        return (pos & m) != 0

    p = lf.reshape(nv, sub, dk)
    for s in (1, 2, 4):
        p = p + jnp.where(pos >= s, later(p, s), 0.0)
    g1 = jnp.where(bit(1), later(p, 1), p)
    h1 = jnp.where(bit(1), p, later(p, sub - 1))
    g2 = jnp.where(bit(2), later(h1, 2), h1)
    h2 = jnp.where(bit(2), h1, later(h1, sub - 2))
    h2_shift = later(h2, 4)
    g4 = jnp.where(bit(4), h2_shift, h2)
    tot8 = jnp.where(bit(4), h2, h2_shift)
    order = range(nv - 1, -1, -1) if reverse else range(nv)
    run = [None] * nv
    b = [None] * nv
    prev = None
    for a in order:
        b[a] = p[a] if prev is None else p[a] + prev
        run[a] = tot8[a] if prev is None else tot8[a] + prev
        prev = run[a]
    tot = prev
    b = jnp.stack(b)
    run = jnp.stack(run)
    q3 = q.reshape(nv, sub, dk)
    k3 = k.reshape(nv, sub, dk)

    rowi = lax.broadcasted_iota(jnp.int32, (c, c), 0)
    coli = lax.broadcasted_iota(jnp.int32, (c, c), 1)
    att = jnp.where(rowi == coli, jnp.sum(q * k, axis=-1, keepdims=True), 0.0)

    def add_level(att, m, x):
        gram = lax.dot_general(x, x, _NT, preferred_element_type=F32)
        r_odd = (rowi & m) != 0
        c_odd = (coli & m) != 0
        same = (rowi & ~(2 * m - 1)) == (coli & ~(2 * m - 1))
        pair = ((~r_odd & c_odd) if reverse else (r_odd & ~c_odd)) & same
        return jnp.where(pair, gram, att)

    for m, g in ((1, g1), (2, g2), (4, g4)):
        z = _neg_abs(p - g)
        x = (jnp.where(bit(m), q3, k3) * jnp.exp2(z)).astype(BF16).reshape(c, dk)
        att = add_level(att, m, x)
    mb = 1
    while mb < nv:
        shp = (nv // (2 * mb), 2, mb, sub, dk)
        first, second = (1, 0) if reverse else (0, 1)
        b5, q5, k5, run5 = (t.reshape(shp) for t in (b, q3, k3, run))
        g = run5[:, first, (0 if reverse else mb - 1)][:, None]
        x_first = k5[:, first] * jnp.exp2(g - b5[:, first])
        x_second = q5[:, second] * jnp.exp2(b5[:, second] - g)
        halves = [x_second, x_first] if reverse else [x_first, x_second]
        x = jnp.stack(halves, axis=1).astype(BF16).reshape(c, dk)
        att = add_level(att, sub * mb, x)
        mb *= 2
    b2 = b.reshape(c, dk)
    q_in = (q * jnp.exp2(b2)).astype(BF16)
    k_out = (k * jnp.exp2(tot[0:1, :] - b2)).astype(BF16)
    return att.astype(BF16), q_in, k_out, tot[0:1, :]


def _scan_inter(att, q_in, k_out, tot, v, st):
    o = lax.dot_general(q_in, st.astype(BF16), _NT, preferred_element_type=F32)
    o = o + jnp.dot(att, v, preferred_element_type=F32)
    st_new = st * jnp.exp2(tot) + lax.dot_general(v, k_out, _TN, preferred_element_type=F32)
    return o, st_new


def _rec_scan_kernel(*refs, reverse):
    if reverse:
        (q_ref, k_ref, lf_ref, v_ref, of_ref, g_ref, gain_ref, h_ref, wo_ref, out_ref, st_ref, att_ref, qin_ref,
         kout_ref, tot_ref, y_ref) = refs
    else:
        q_ref, k_ref, lf_ref, v_ref, of_ref, st_ref, att_ref, qin_ref, kout_ref, tot_ref = refs

    @pl.when(pl.program_id(1) == 0)
    def _():
        st_ref[...] = jnp.zeros_like(st_ref)

    n_chunks = SCAN_BLOCK // SCAN_CHUNK
    chunk_order = list(reversed(range(n_chunks)) if reverse else range(n_chunks))
    def intra(cc, hd):
        rows = slice(cc * SCAN_CHUNK, (cc + 1) * SCAN_CHUNK)
        cols = slice(hd * REC_EXPAND, (hd + 1) * REC_EXPAND)
        att_ref[cc, hd], qin_ref[rows, cols], kout_ref[rows, cols], tot_ref[cc, :, cols] = _scan_intra(
            q_ref[0, rows, cols], k_ref[0, rows, cols], lf_ref[0, rows, cols], reverse=reverse)

    def inter(cc, hd):
        rows = slice(cc * SCAN_CHUNK, (cc + 1) * SCAN_CHUNK)
        cols = slice(hd * REC_EXPAND, (hd + 1) * REC_EXPAND)
        o, st_new = _scan_inter(att_ref[cc, hd], qin_ref[rows, cols], kout_ref[rows, cols],
                                tot_ref[cc, :, cols], v_ref[0, rows, cols], st_ref[hd])
        st_ref[hd] = st_new
        if reverse:
            o = o + of_ref[0, rows, cols].astype(F32)
            g = g_ref[0, rows, cols].astype(F32)
            y = o * _rms_scale(o) * gain_ref[...] * (g * jax.nn.sigmoid(g))
            y_ref[rows, cols] = y.astype(BF16)
        else:
            of_ref[0, rows, cols] = o.astype(BF16)

    for idx, cc in enumerate(chunk_order):
        for hd in range(REC_HEADS):
            intra(cc, hd)
            if idx > 0:
                inter(chunk_order[idx - 1], hd)
    for hd in range(REC_HEADS):
        inter(chunk_order[-1], hd)
    if reverse:
        out_ref[0] = h_ref[0] + jnp.dot(y_ref[...], wo_ref[...], preferred_element_type=F32)


def _rec_scan(q, k, lf, v, tail=None, *, reverse):
    b, s, _ = q.shape
    nblk = s // SCAN_BLOCK
    blk_of = (lambda j: nblk - 1 - j) if reverse else (lambda j: j)
    blk = pl.BlockSpec((1, SCAN_BLOCK, D_MODEL), lambda bi, j: (bi, blk_of(j), 0))
    in_specs = [blk, blk, blk, blk]
    n_chunks = SCAN_BLOCK // SCAN_CHUNK
    scratch = [pltpu.VMEM((REC_HEADS, REC_V_DIM, REC_EXPAND), F32),
               pltpu.VMEM((n_chunks, REC_HEADS, SCAN_CHUNK, SCAN_CHUNK), BF16),
               pltpu.VMEM((SCAN_BLOCK, D_MODEL), BF16), pltpu.VMEM((SCAN_BLOCK, D_MODEL), BF16),
               pltpu.VMEM((n_chunks, 1, D_MODEL), F32)]
    args = (q, k, lf, v)
    if reverse:
        in_specs += [blk, blk, _const_spec((1, REC_V_DIM)), blk, _const_spec((D_MODEL, D_MODEL))]
        scratch += [pltpu.VMEM((SCAN_BLOCK, D_MODEL), BF16)]
        args += tuple(tail)
    return pl.pallas_call(
        functools.partial(_rec_scan_kernel, reverse=reverse),
        grid=(b, nblk),
        in_specs=in_specs,
        out_specs=blk,
        out_shape=jax.ShapeDtypeStruct((b, s, D_MODEL), F32 if reverse else BF16),
        scratch_shapes=scratch,
        compiler_params=pltpu.CompilerParams(dimension_semantics=("parallel", "arbitrary"),
                                             vmem_limit_bytes=VMEM_LIMIT),
        name="rec_scan_bwd_out" if reverse else "rec_scan_fwd",
    )(*args)


def _trunk(x, p):
    b, s, d = x.shape
    t = b * s
    x = x.reshape(t, d)
    for layer in range(DEPTH):
        j = layer // N_MIXERS
        h, hn = _ffn(x, p["ffn1_norm"][layer], p["ffn1_w_gate"][layer], p["ffn1_w_up"][layer],
                     p["ffn1_w_down"][layer], p["mix_norm"][layer], final=False)
        if layer % N_MIXERS == 0:
            q, k, v = _attn_qkv(hn, p["attn_w_in"][j], p["attn_gq"][j], p["attn_gk"][j])
            h = _attn_core(q, k, v.reshape(b, s, KV_DIM), p["attn_sink"][j], h.reshape(b, s, d),
                           p["attn_w_out"][j]).reshape(t, d)
        else:
            q, lf_fw, lf_bw, k_fw, k_bw, iv, g = _rec_proj(hn, p["rec_w_in"][j], p["lb"][j])
            r3 = lambda a: a.reshape(b, s, d)
            o_fw = _rec_scan(r3(q), r3(k_fw), r3(lf_fw), r3(iv), reverse=False)
            tail = (o_fw, r3(g), p["rec_o_gain"][j], r3(h), p["rec_w_out"][j])
            h = _rec_scan(r3(q), r3(k_bw), r3(lf_bw), r3(iv), tail, reverse=True).reshape(t, d)
        x = _ffn(h, p["ffn2_norm"][layer], p["ffn2_w_gate"][layer], p["ffn2_w_up"][layer],
                 p["ffn2_w_down"][layer], p["out_norm"][layer], final=True)
    return x.reshape(b, s, d)


def kernel(x_prompt, x_sample, ffn1_norm, ffn1_w_gate, ffn1_w_up, ffn1_w_down, mix_norm, attn_w_in, attn_q_gain,
           attn_k_gain, attn_sink, attn_w_out, rec_w_in, rec_lb_logits, rec_o_gain, rec_w_out, ffn2_norm,
           ffn2_w_gate, ffn2_w_up, ffn2_w_down, out_norm):
    lb = jnp.cumsum(jax.nn.softmax(rec_lb_logits.astype(F32), axis=0), axis=0)
    lb = lb - lb[0:1]
    row = lambda a: a.astype(F32)[:, None, :]
    p = {
        "ffn1_norm": row(ffn1_norm), "mix_norm": row(mix_norm), "ffn2_norm": row(ffn2_norm),
        "out_norm": row(out_norm),
        "ffn1_w_gate": ffn1_w_gate.astype(BF16), "ffn1_w_up": ffn1_w_up.astype(BF16),
        "ffn1_w_down": ffn1_w_down.astype(BF16),
        "ffn2_w_gate": ffn2_w_gate.astype(BF16), "ffn2_w_up": ffn2_w_up.astype(BF16),
        "ffn2_w_down": ffn2_w_down.astype(BF16),
        "attn_w_in": attn_w_in.astype(BF16), "attn_w_out": attn_w_out.astype(BF16),
        "attn_gq": jnp.tile(attn_q_gain.astype(F32), (1, N_Q_HEADS))[:, None, :] * (HEAD_DIM ** -0.5 * LOG2E),
        "attn_gk": jnp.tile(attn_k_gain.astype(F32), (1, N_KV_HEADS))[:, None, :],
        "attn_sink": attn_sink.astype(F32),
        "rec_w_in": rec_w_in.astype(BF16), "rec_w_out": rec_w_out.astype(BF16),
        "rec_o_gain": row(rec_o_gain), "lb": lb,
    }
    return (_trunk(x_prompt, p), _trunk(x_sample, p))
```

```python
import functools

import numpy as np
import jax
import jax.numpy as jnp
from jax import lax
from jax.experimental import pallas as pl
from jax.experimental.pallas import tpu as pltpu

D_MODEL = 1024
DEPTH = 4
N_MIXERS = 2
HEAD_DIM = 64
N_Q_HEADS = D_MODEL // HEAD_DIM
N_KV_HEADS = N_Q_HEADS // 4
GQA_GROUP = N_Q_HEADS // N_KV_HEADS
KV_DIM = N_KV_HEADS * HEAD_DIM
WINDOW = 128
ATTN_BLOCK = 128
ATTN_STEP_BLOCKS = 4
ATTN_LOOKAHEAD = 1
REC_EXPAND = 128
REC_HEADS = D_MODEL // REC_EXPAND
REC_V_DIM = D_MODEL // REC_HEADS
D_FF = 2816
EPS = 1e-6

F32 = jnp.float32
BF16 = jnp.bfloat16

FFN_TM = 1024
FFN_ROWS = 512
FFN_CHUNK = 256
QKV_TM = 1024
REC_PROJ_TM = 512
REC_PROJ_CHUNK = 256
SCAN_CHUNK = 128
SCAN_BLOCK = 512
VMEM_LIMIT = 56 * 1024 * 1024

_NT = (((1,), (1,)), ((), ()))
_TN = (((0,), (0,)), ((), ()))


def _const_spec(shape):
    return pl.BlockSpec(shape, lambda *_: (0,) * len(shape), pipeline_mode=pl.Buffered(1))


def _rms_scale(x):
    return lax.rsqrt(jnp.mean(x * x, axis=-1, keepdims=True) + EPS)


def _ffn_kernel(x_ref, g1_ref, wg_ref, wu_ref, wd_ref, g2_ref, *rest, final):
    if final:
        y_ref, acc_ref = rest
    else:
        h_ref, hn_ref, acc_ref = rest
    n_groups = max(1, x_ref.shape[0] // FFN_ROWS)
    for grp in range(n_groups):
        rows = slice(grp * (x_ref.shape[0] // n_groups), (grp + 1) * (x_ref.shape[0] // n_groups))
        x = x_ref[rows]
        n = (x * _rms_scale(x) * g1_ref[...]).astype(BF16)
        for c in range(D_FF // FFN_CHUNK):
            sl = slice(c * FFN_CHUNK, (c + 1) * FFN_CHUNK)
            g = jnp.dot(n, wg_ref[:, sl], preferred_element_type=F32)
            u = jnp.dot(n, wu_ref[:, sl], preferred_element_type=F32)
            a = (g * jax.nn.sigmoid(g) * u).astype(BF16)
            d = jnp.dot(a, wd_ref[sl, :], preferred_element_type=F32)
            if c == 0:
                acc_ref[rows] = d
            else:
                acc_ref[rows] += d
        h = x + 0.5 * acc_ref[rows]
        hn = h * _rms_scale(h) * g2_ref[...]
        if final:
            y_ref[rows] = hn
        else:
            h_ref[rows] = h
            hn_ref[rows] = hn.astype(BF16)


def _ffn(x, g1, wg, wu, wd, g2, *, final):
    t = x.shape[0]
    tm = min(FFN_TM, t)
    row = pl.BlockSpec((tm, D_MODEL), lambda i: (i, 0))
    if final:
        out_shape = jax.ShapeDtypeStruct((t, D_MODEL), F32)
        out_specs = row
    else:
        out_shape = (jax.ShapeDtypeStruct((t, D_MODEL), F32), jax.ShapeDtypeStruct((t, D_MODEL), BF16))
        out_specs = (row, row)
    return pl.pallas_call(
        functools.partial(_ffn_kernel, final=final),
        grid=(t // tm,),
        in_specs=[row, _const_spec((1, D_MODEL)), _const_spec((D_MODEL, D_FF)), _const_spec((D_MODEL, D_FF)),
                  _const_spec((D_FF, D_MODEL)), _const_spec((1, D_MODEL))],
        out_specs=out_specs,
        out_shape=out_shape,
        scratch_shapes=[pltpu.VMEM((tm, D_MODEL), F32)],
        compiler_params=pltpu.CompilerParams(dimension_semantics=("parallel",), vmem_limit_bytes=VMEM_LIMIT),
        name="ffn_final" if final else "ffn_mid",
    )(x, g1, wg, wu, wd, g2)


def _group_mean_matrix(n, group):
    idx = np.arange(n) // group
    return jnp.asarray((idx[:, None] == idx[None, :]).astype(np.float32) / group, dtype=BF16)


def _qkv_kernel(hn_ref, w_ref, wvt_ref, gq_ref, gk_ref, gm_ref, q_ref, k_ref, vt_ref):
    hn = hn_ref[...]
    gm = gm_ref[...]
    qd = N_Q_HEADS * HEAD_DIM

    def proj(c):
        return jnp.dot(hn, w_ref[:, c * KV_DIM:(c + 1) * KV_DIM], preferred_element_type=F32)

    def store_normed(c, x):
        is_q = c < qd // KV_DIM
        gain = gq_ref[:, c * KV_DIM:(c + 1) * KV_DIM] if is_q else gk_ref[...]
        ms = jnp.dot((x * x).astype(BF16), gm, preferred_element_type=F32)
        y = (x * lax.rsqrt(ms + EPS) * gain).astype(BF16)
        out_ref, head0 = (q_ref, c * N_KV_HEADS) if is_q else (k_ref, 0)
        for j in range(N_KV_HEADS):
            for rb in range(out_ref.shape[0]):
                out_ref[rb, head0 + j] = y[rb * ATTN_BLOCK:(rb + 1) * ATTN_BLOCK, j * HEAD_DIM:(j + 1) * HEAD_DIM]

    n_norm = (qd + KV_DIM) // KV_DIM
    x_next = proj(0)
    for c in range(n_norm):
        x = x_next
        if c + 1 < n_norm:
            x_next = proj(c + 1)
        else:
            vt = lax.dot_general(wvt_ref[...], hn, _NT, preferred_element_type=F32).astype(BF16)
            for g in range(N_KV_HEADS):
                for rb in range(vt_ref.shape[0]):
                    vt_ref[rb, g] = vt[g * HEAD_DIM:(g + 1) * HEAD_DIM, rb * ATTN_BLOCK:(rb + 1) * ATTN_BLOCK]
        store_normed(c, x)


def _attn_qkv(hn, w_in, w_vt, gq, gk):
    t = hn.shape[0]
    tm = min(QKV_TM, t)
    nrb = tm // ATTN_BLOCK
    in_dim = w_in.shape[1]
    blocks = lambda *dims: pl.BlockSpec((nrb,) + dims, lambda i: (i, 0, 0, 0))
    shape = lambda *dims: jax.ShapeDtypeStruct((t // ATTN_BLOCK,) + dims, BF16)
    return pl.pallas_call(
        _qkv_kernel,
        grid=(t // tm,),
        in_specs=[pl.BlockSpec((tm, D_MODEL), lambda i: (i, 0)), _const_spec((D_MODEL, in_dim)),
                  _const_spec((KV_DIM, D_MODEL)), _const_spec((1, D_MODEL)), _const_spec((1, KV_DIM)),
                  _const_spec((KV_DIM, KV_DIM))],
        out_specs=(blocks(N_Q_HEADS, ATTN_BLOCK, HEAD_DIM), blocks(N_KV_HEADS, ATTN_BLOCK, HEAD_DIM),
                   blocks(N_KV_HEADS, HEAD_DIM, ATTN_BLOCK)),
        out_shape=(shape(N_Q_HEADS, ATTN_BLOCK, HEAD_DIM), shape(N_KV_HEADS, ATTN_BLOCK, HEAD_DIM),
                   shape(N_KV_HEADS, HEAD_DIM, ATTN_BLOCK)),
        compiler_params=pltpu.CompilerParams(dimension_semantics=("parallel",), vmem_limit_bytes=VMEM_LIMIT),
        name="attn_qkv",
    )(hn, w_in, w_vt, gq, gk, _group_mean_matrix(KV_DIM, HEAD_DIM))


LOG2E = float(np.log2(np.e))
_ALIBI_SLOPES = [float(2.0 ** (-8.0 * (i + 1) / N_Q_HEADS)) for i in range(N_Q_HEADS)]
_MASKED = -1e30
_ONES_ROWS = 16


def _per_head_cols(values, cols_per_head, shape):
    col = lax.broadcasted_iota(jnp.int32, shape, 1)
    out = jnp.full(shape, values[-1], F32)
    for j in range(len(values) - 2, -1, -1):
        out = jnp.where(col < (j + 1) * cols_per_head, values[j], out)
    return out


def _attn_core_kernel(sink_ref, q_ref, kp_ref, kc_ref, kn_ref, vp_ref, vc_ref, vn_ref, h_ref, wo_ref,
                      out_ref, att_ref, bias_ref, *, n_steps):
    n = pl.program_id(1)
    blk = ATTN_BLOCK
    nq = ATTN_STEP_BLOCKS
    kb = 3 * blk
    gm = GQA_GROUP * blk

    @pl.when(n == 0)
    def _():
        key = lax.broadcasted_iota(jnp.int32, (kb, gm), 0)
        qry = lax.broadcasted_iota(jnp.int32, (kb, gm), 1) & (blk - 1)
        absd = jnp.abs(key - WINDOW - qry)
        absd_f = absd.astype(F32)
        for g in range(N_KV_HEADS):
            slopes = [_ALIBI_SLOPES[g * GQA_GROUP + j] * LOG2E for j in range(GQA_GROUP)]
            bias_ref[g] = jnp.where(absd <= WINDOW, -_per_head_cols(slopes, blk, (kb, gm)) * absd_f, _MASKED)

    def k_block(j, g):
        return kp_ref[0, g] if j < 0 else kn_ref[0, g] if j == nq else kc_ref[j, g]

    def vt_block(j, g):
        return vp_ref[0, g] if j < 0 else vn_ref[0, g] if j == nq else vc_ref[j, g]

    key1 = lax.broadcasted_iota(jnp.int32, (kb, 1), 0)
    ones = jnp.ones((_ONES_ROWS, kb), BF16)

    def scores(i, g):
        qg = jnp.concatenate([q_ref[i, g * GQA_GROUP + j] for j in range(GQA_GROUP)], axis=0)
        kg = jnp.concatenate([k_block(j, g) for j in (i - 1, i, i + 1)], axis=0)
        s = lax.dot_general(kg, qg, _NT, preferred_element_type=F32) + bias_ref[g]
        if i == 0:
            s = s + jnp.where((key1 < blk) & (n == 0), _MASKED, 0.0)
        if i == nq - 1:
            s = s + jnp.where((key1 >= 2 * blk) & (n == n_steps - 1), _MASKED, 0.0)
        return s

    def attend(i, g, s):
        sink = _per_head_cols([sink_ref[g * GQA_GROUP + j] * LOG2E for j in range(GQA_GROUP)], blk, (1, gm))
        m = jnp.maximum(jnp.max(s, axis=0, keepdims=True), sink)
        p = jnp.exp2(s - m).astype(BF16)
        vt = jnp.concatenate([vt_block(j, g) for j in (i - 1, i, i + 1)], axis=1)
        o = jnp.dot(jnp.concatenate([vt, ones], axis=0), p, preferred_element_type=F32)
        a = (o[:HEAD_DIM] / (o[HEAD_DIM:HEAD_DIM + 1] + jnp.exp2(sink - m))).astype(BF16)
        for j in range(GQA_GROUP):
            hd = g * GQA_GROUP + j
            att_ref[hd * HEAD_DIM:(hd + 1) * HEAD_DIM, i * blk:(i + 1) * blk] = a[:, j * blk:(j + 1) * blk]

    pending = []
    for i in range(nq):
        for g in range(N_KV_HEADS):
            pending.append((i, g, scores(i, g)))
            if len(pending) > ATTN_LOOKAHEAD:
                attend(*pending.pop(0))
    for item in pending:
        attend(*item)
    out_ref[0] = h_ref[0] + lax.dot_general(att_ref[...], wo_ref[...], _TN, preferred_element_type=F32)


def _attn_core(q, k, vt, sink, h, w_out):
    b, s, _ = h.shape
    nq = ATTN_STEP_BLOCKS
    nb = s // ATTN_BLOCK
    ns = nb // nq
    prev_blk = lambda n: jnp.maximum(n * nq - 1, 0)
    next_blk = lambda n: jnp.minimum(n * nq + nq, nb - 1)
    blocks = lambda dims, rows_of: pl.BlockSpec((rows_of[0],) + dims, lambda bi, n: (rows_of[1](bi, n), 0, 0, 0))
    mid = (nq, lambda bi, n: bi * ns + n)
    before = (1, lambda bi, n: bi * nb + prev_blk(n))
    after = (1, lambda bi, n: bi * nb + next_blk(n))
    q_dims = (N_Q_HEADS, ATTN_BLOCK, HEAD_DIM)
    k_dims = (N_KV_HEADS, ATTN_BLOCK, HEAD_DIM)
    v_dims = (N_KV_HEADS, HEAD_DIM, ATTN_BLOCK)
    step_rows = pl.BlockSpec((1, nq * ATTN_BLOCK, D_MODEL), lambda bi, n: (bi, n, 0))
    return pl.pallas_call(
        functools.partial(_attn_core_kernel, n_steps=ns),
        grid=(b, ns),
        in_specs=[pl.BlockSpec(memory_space=pltpu.SMEM),
                  blocks(q_dims, mid),
                  blocks(k_dims, before), blocks(k_dims, mid), blocks(k_dims, after),
                  blocks(v_dims, before), blocks(v_dims, mid), blocks(v_dims, after),
                  step_rows, _const_spec((D_MODEL, D_MODEL))],
        out_specs=step_rows,
        out_shape=jax.ShapeDtypeStruct((b, s, D_MODEL), F32),
        scratch_shapes=[pltpu.VMEM((D_MODEL, nq * ATTN_BLOCK), BF16),
                        pltpu.VMEM((N_KV_HEADS, 3 * ATTN_BLOCK, GQA_GROUP * ATTN_BLOCK), F32)],
        compiler_params=pltpu.CompilerParams(dimension_semantics=("arbitrary", "arbitrary"),
                                             vmem_limit_bytes=VMEM_LIMIT),
        name="attn_core",
    )(sink, q, k, k, k, vt, vt, vt, h, w_out)


def _forget_gate(z2, lb):
    log_lb = jnp.log2(lb)
    log_1m_lb = jnp.log1p(-lb) * LOG2E
    e = jnp.exp2(-jnp.abs(z2))
    ope = 1.0 + e
    cl = log_1m_lb + (jnp.minimum(z2, 0.0) - jnp.log2(ope))
    log_f = jnp.maximum(log_lb, cl) + jnp.log2(1.0 + jnp.exp2(-jnp.abs(log_lb - cl)))
    return log_f, (1.0 - lb) * (jnp.where(z2 >= 0.0, e, 1.0) / ope)


def _rec_proj_kernel(hn_ref, w_ref, lb_ref, q_ref, lf_fw_ref, lf_bw_ref, k_fw_ref, k_bw_ref, i_ref, g_ref, *z_refs):
    hn = hn_ref[...]
    f = D_MODEL
    cw = REC_PROJ_CHUNK
    n_chunks = f // cw

    def proj(col):
        return jnp.dot(hn, w_ref[:, col:col + cw], preferred_element_type=F32)

    def parked(c, d):
        return z_refs[c % 2].at[lax.rem(pl.program_id(0) + c // 2, 2), d]

    def project(c):
        sl = slice(c * cw, (c + 1) * cw)
        q_ref[:, sl] = proj(c * cw)
        parked(c, 0)[...] = proj(f + c * cw)
        parked(c, 1)[...] = proj(2 * f + c * cw)
        i_ref[:, sl] = proj(3 * f + c * cw).astype(BF16)
        g_ref[:, sl] = proj(4 * f + c * cw).astype(BF16)

    def gates(c):
        sl = slice(c * cw, (c + 1) * cw)
        lf_fw_ref[:, sl], k_fw_ref[:, sl] = _forget_gate(parked(c, 0)[...] * LOG2E, lb_ref[0:1, sl])
        lf_bw_ref[:, sl], k_bw_ref[:, sl] = _forget_gate(parked(c, 1)[...] * LOG2E, lb_ref[1:2, sl])

    project(0)
    for c in range(n_chunks):
        if c + 1 < n_chunks:
            project(c + 1)
        gates(c)


def _rec_proj(hn, w_in, lb):
    t = hn.shape[0]
    tm = min(REC_PROJ_TM, t)
    row = pl.BlockSpec((tm, D_MODEL), lambda i: (i, 0))
    f32o = jax.ShapeDtypeStruct((t, D_MODEL), F32)
    bf16o = jax.ShapeDtypeStruct((t, D_MODEL), BF16)
    return pl.pallas_call(
        _rec_proj_kernel,
        grid=(t // tm,),
        in_specs=[row, _const_spec((D_MODEL, w_in.shape[1])), _const_spec((2, D_MODEL))],
        out_specs=(row,) * 7,
        out_shape=(f32o, f32o, f32o, f32o, f32o, bf16o, bf16o),
        scratch_shapes=[pltpu.VMEM((2, 2, tm, REC_PROJ_CHUNK), F32)] * 2,
        compiler_params=pltpu.CompilerParams(dimension_semantics=("parallel",), vmem_limit_bytes=VMEM_LIMIT),
        name="rec_proj",
    )(hn, w_in, lb)


def _neg_abs(x):
    bits = lax.bitcast_convert_type(x, jnp.uint32) | jnp.uint32(0x80000000)
    return lax.bitcast_convert_type(bits, F32)


def _scan_intra(q, k, lf, *, reverse):
    c, dk = q.shape
    sub = 8
    nv = c // sub
    r = lax.broadcasted_iota(jnp.int32, (nv, sub, dk), 1)
    pos = (sub - 1 - r) if reverse else r

    def later(x, s):
        return pltpu.roll(x, (sub - s) if reverse else s, 1)

    def bit(m):
        return (pos & m) != 0

    p = lf.reshape(nv, sub, dk)
    for s in (1, 2, 4):
        p = p + jnp.where(pos >= s, later(p, s), 0.0)
    g1 = jnp.where(bit(1), later(p, 1), p)
    h1 = jnp.where(bit(1), p, later(p, sub - 1))
    g2 = jnp.where(bit(2), later(h1, 2), h1)
    h2 = jnp.where(bit(2), h1, later(h1, sub - 2))
    h2_shift = later(h2, 4)
    g4 = jnp.where(bit(4), h2_shift, h2)
    tot8 = jnp.where(bit(4), h2, h2_shift)
    order = range(nv - 1, -1, -1) if reverse else range(nv)
    run = [None] * nv
    b = [None] * nv
    prev = None
    for a in order:
        b[a] = p[a] if prev is None else p[a] + prev
        run[a] = tot8[a] if prev is None else tot8[a] + prev
        prev = run[a]
    tot = prev
    b = jnp.stack(b)
    run = jnp.stack(run)
    q3 = q.reshape(nv, sub, dk)
    k3 = k.reshape(nv, sub, dk)

    rowi = lax.broadcasted_iota(jnp.int32, (c, c), 0)
    coli = lax.broadcasted_iota(jnp.int32, (c, c), 1)
    att = jnp.where(rowi == coli, jnp.sum(q * k, axis=-1, keepdims=True), 0.0)

    def add_level(att, m, x):
        gram = lax.dot_general(x, x, _NT, preferred_element_type=F32)
        r_odd = (rowi & m) != 0
        c_odd = (coli & m) != 0
        same = (rowi & ~(2 * m - 1)) == (coli & ~(2 * m - 1))
        pair = ((~r_odd & c_odd) if reverse else (r_odd & ~c_odd)) & same
        return jnp.where(pair, gram, att)

    for m, g in ((1, g1), (2, g2), (4, g4)):
        z = _neg_abs(p - g)
        x = (jnp.where(bit(m), q3, k3) * jnp.exp2(z)).astype(BF16).reshape(c, dk)
        att = add_level(att, m, x)
    mb = 1
    while mb < nv:
        shp = (nv // (2 * mb), 2, mb, sub, dk)
        first, second = (1, 0) if reverse else (0, 1)
        b5, q5, k5, run5 = (t.reshape(shp) for t in (b, q3, k3, run))
        g = run5[:, first, (0 if reverse else mb - 1)][:, None]
        x_first = k5[:, first] * jnp.exp2(g - b5[:, first])
        x_second = q5[:, second] * jnp.exp2(b5[:, second] - g)
        halves = [x_second, x_first] if reverse else [x_first, x_second]
        x = jnp.stack(halves, axis=1).astype(BF16).reshape(c, dk)
        att = add_level(att, sub * mb, x)
        mb *= 2
    b2 = b.reshape(c, dk)
    q_in = (q * jnp.exp2(b2)).astype(BF16)
    k_out = (k * jnp.exp2(tot[0:1, :] - b2)).astype(BF16)
    return att.astype(BF16), q_in, k_out, tot[0:1, :]


def _scan_inter(att, q_in, k_out, tot, v, st):
    o = lax.dot_general(q_in, st.astype(BF16), _NT, preferred_element_type=F32)
    o = o + jnp.dot(att, v, preferred_element_type=F32)
    st_new = st * jnp.exp2(tot) + lax.dot_general(v, k_out, _TN, preferred_element_type=F32)
    return o, st_new


def _rec_scan_kernel(*refs, reverse):
    if reverse:
        (q_ref, k_ref, lf_ref, v_ref, of_ref, g_ref, gain_ref, h_ref, wo_ref, out_ref, st_ref, att_ref, qin_ref,
         kout_ref, tot_ref, y_ref) = refs
    else:
        q_ref, k_ref, lf_ref, v_ref, of_ref, st_ref, att_ref, qin_ref, kout_ref, tot_ref = refs

    @pl.when(pl.program_id(1) == 0)
    def _():
        st_ref[...] = jnp.zeros_like(st_ref)

    n_chunks = SCAN_BLOCK // SCAN_CHUNK
    chunk_order = list(reversed(range(n_chunks)) if reverse else range(n_chunks))
    def intra(cc, hd):
        rows = slice(cc * SCAN_CHUNK, (cc + 1) * SCAN_CHUNK)
        cols = slice(hd * REC_EXPAND, (hd + 1) * REC_EXPAND)
        att_ref[cc, hd], qin_ref[rows, cols], kout_ref[rows, cols], tot_ref[cc, :, cols] = _scan_intra(
            q_ref[0, rows, cols], k_ref[0, rows, cols], lf_ref[0, rows, cols], reverse=reverse)

    def inter(cc, hd):
        rows = slice(cc * SCAN_CHUNK, (cc + 1) * SCAN_CHUNK)
        cols = slice(hd * REC_EXPAND, (hd + 1) * REC_EXPAND)
        o, st_new = _scan_inter(att_ref[cc, hd], qin_ref[rows, cols], kout_ref[rows, cols],
                                tot_ref[cc, :, cols], v_ref[0, rows, cols], st_ref[hd])
        st_ref[hd] = st_new
        if reverse:
            o = o + of_ref[0, rows, cols].astype(F32)
            g = g_ref[0, rows, cols].astype(F32)
            y = o * _rms_scale(o) * gain_ref[...] * (g * jax.nn.sigmoid(g))
            y_ref[rows, cols] = y.astype(BF16)
        else:
            of_ref[0, rows, cols] = o.astype(BF16)

    for idx, cc in enumerate(chunk_order):
        for hd in range(REC_HEADS):
            intra(cc, hd)
            if idx > 0:
                inter(chunk_order[idx - 1], hd)
    for hd in range(REC_HEADS):
        inter(chunk_order[-1], hd)
    if reverse:
        out_ref[0] = h_ref[0] + jnp.dot(y_ref[...], wo_ref[...], preferred_element_type=F32)


def _rec_scan(q, k, lf, v, tail=None, *, reverse):
    b, s, _ = q.shape
    nblk = s // SCAN_BLOCK
    blk_of = (lambda j: nblk - 1 - j) if reverse else (lambda j: j)
    blk = pl.BlockSpec((1, SCAN_BLOCK, D_MODEL), lambda bi, j: (bi, blk_of(j), 0))
    in_specs = [blk, blk, blk, blk]
    n_chunks = SCAN_BLOCK // SCAN_CHUNK
    scratch = [pltpu.VMEM((REC_HEADS, REC_V_DIM, REC_EXPAND), F32),
               pltpu.VMEM((n_chunks, REC_HEADS, SCAN_CHUNK, SCAN_CHUNK), BF16),
               pltpu.VMEM((SCAN_BLOCK, D_MODEL), BF16), pltpu.VMEM((SCAN_BLOCK, D_MODEL), BF16),
               pltpu.VMEM((n_chunks, 1, D_MODEL), F32)]
    args = (q, k, lf, v)
    if reverse:
        in_specs += [blk, blk, _const_spec((1, REC_V_DIM)), blk, _const_spec((D_MODEL, D_MODEL))]
        scratch += [pltpu.VMEM((SCAN_BLOCK, D_MODEL), BF16)]
        args += tuple(tail)
    return pl.pallas_call(
        functools.partial(_rec_scan_kernel, reverse=reverse),
        grid=(b, nblk),
        in_specs=in_specs,
        out_specs=blk,
        out_shape=jax.ShapeDtypeStruct((b, s, D_MODEL), F32 if reverse else BF16),
        scratch_shapes=scratch,
        compiler_params=pltpu.CompilerParams(dimension_semantics=("parallel", "arbitrary"),
                                             vmem_limit_bytes=VMEM_LIMIT),
        name="rec_scan_bwd_out" if reverse else "rec_scan_fwd",
    )(*args)


def _trunk(x, p):
    b, s, d = x.shape
    t = b * s
    x = x.reshape(t, d)
    for layer in range(DEPTH):
        j = layer // N_MIXERS
        h, hn = _ffn(x, p["ffn1_norm"][layer], p["ffn1_w_gate"][layer], p["ffn1_w_up"][layer],
                     p["ffn1_w_down"][layer], p["mix_norm"][layer], final=False)
        if layer % N_MIXERS == 0:
            q, k, vt = _attn_qkv(hn, p["attn_w_in"][j], p["attn_w_vt"][j], p["attn_gq"][j], p["attn_gk"][j])
            h = _attn_core(q, k, vt, p["attn_sink"][j], h.reshape(b, s, d), p["attn_w_out"][j]).reshape(t, d)
        else:
            q, lf_fw, lf_bw, k_fw, k_bw, iv, g = _rec_proj(hn, p["rec_w_in"][j], p["lb"][j])
            r3 = lambda a: a.reshape(b, s, d)
            o_fw = _rec_scan(r3(q), r3(k_fw), r3(lf_fw), r3(iv), reverse=False)
            tail = (o_fw, r3(g), p["rec_o_gain"][j], r3(h), p["rec_w_out"][j])
            h = _rec_scan(r3(q), r3(k_bw), r3(lf_bw), r3(iv), tail, reverse=True).reshape(t, d)
        x = _ffn(h, p["ffn2_norm"][layer], p["ffn2_w_gate"][layer], p["ffn2_w_up"][layer],
                 p["ffn2_w_down"][layer], p["out_norm"][layer], final=True)
    return x.reshape(b, s, d)


def kernel(x_prompt, x_sample, ffn1_norm, ffn1_w_gate, ffn1_w_up, ffn1_w_down, mix_norm, attn_w_in, attn_q_gain,
           attn_k_gain, attn_sink, attn_w_out, rec_w_in, rec_lb_logits, rec_o_gain, rec_w_out, ffn2_norm,
           ffn2_w_gate, ffn2_w_up, ffn2_w_down, out_norm):
    lb = jnp.cumsum(jax.nn.softmax(rec_lb_logits.astype(F32), axis=0), axis=0)
    lb = lb - lb[0:1]
    row = lambda a: a.astype(F32)[:, None, :]
    p = {
        "ffn1_norm": row(ffn1_norm), "mix_norm": row(mix_norm), "ffn2_norm": row(ffn2_norm),
        "out_norm": row(out_norm),
        "ffn1_w_gate": ffn1_w_gate.astype(BF16), "ffn1_w_up": ffn1_w_up.astype(BF16),
        "ffn1_w_down": ffn1_w_down.astype(BF16),
        "ffn2_w_gate": ffn2_w_gate.astype(BF16), "ffn2_w_up": ffn2_w_up.astype(BF16),
        "ffn2_w_down": ffn2_w_down.astype(BF16),
        "attn_w_in": attn_w_in.astype(BF16), "attn_w_out": attn_w_out.astype(BF16),
        "attn_w_vt": jnp.swapaxes(attn_w_in[:, :, D_MODEL + KV_DIM:], 1, 2).astype(BF16),
        "attn_gq": jnp.tile(attn_q_gain.astype(F32), (1, N_Q_HEADS))[:, None, :] * (HEAD_DIM ** -0.5 * LOG2E),
        "attn_gk": jnp.tile(attn_k_gain.astype(F32), (1, N_KV_HEADS))[:, None, :],
        "attn_sink": attn_sink.astype(F32),
        "rec_w_in": rec_w_in.astype(BF16), "rec_w_out": rec_w_out.astype(BF16),
        "rec_o_gain": row(rec_o_gain), "lb": lb,
    }
    return (_trunk(x_prompt, p), _trunk(x_sample, p))
```

```python
import functools

import numpy as np
import jax
import jax.numpy as jnp
from jax import lax
from jax.experimental import pallas as pl
from jax.experimental.pallas import tpu as pltpu

D_MODEL = 1024
DEPTH = 4
N_MIXERS = 2
HEAD_DIM = 64
N_Q_HEADS = D_MODEL // HEAD_DIM
N_KV_HEADS = N_Q_HEADS // 4
GQA_GROUP = N_Q_HEADS // N_KV_HEADS
KV_DIM = N_KV_HEADS * HEAD_DIM
WINDOW = 128
ATTN_BLOCK = 128
ATTN_STEP_BLOCKS = 8
ATTN_LOOKAHEAD = 1
REC_EXPAND = 128
REC_HEADS = D_MODEL // REC_EXPAND
REC_V_DIM = D_MODEL // REC_HEADS
D_FF = 2816
EPS = 1e-6

F32 = jnp.float32
BF16 = jnp.bfloat16

FFN_TM = 1024
FFN_ROWS = 512
FFN_CHUNK = 256
QKV_TM = 1024
REC_PROJ_TM = 512
REC_PROJ_CHUNK = 256
SCAN_CHUNK = 128
SCAN_BLOCK = 512
VMEM_LIMIT = 56 * 1024 * 1024

_NT = (((1,), (1,)), ((), ()))
_TN = (((0,), (0,)), ((), ()))


def _const_spec(shape):
    return pl.BlockSpec(shape, lambda *_: (0,) * len(shape), pipeline_mode=pl.Buffered(1))


def _rms_scale(x):
    return lax.rsqrt(jnp.mean(x * x, axis=-1, keepdims=True) + EPS)


def _ffn_kernel(x_ref, g1_ref, wg_ref, wu_ref, wd_ref, g2_ref, *rest, final):
    if final:
        y_ref, acc_ref = rest
    else:
        h_ref, hn_ref, acc_ref = rest
    n_groups = max(1, x_ref.shape[0] // FFN_ROWS)
    for grp in range(n_groups):
        rows = slice(grp * (x_ref.shape[0] // n_groups), (grp + 1) * (x_ref.shape[0] // n_groups))
        x = x_ref[rows]
        n = (x * _rms_scale(x) * g1_ref[...]).astype(BF16)
        for c in range(D_FF // FFN_CHUNK):
            sl = slice(c * FFN_CHUNK, (c + 1) * FFN_CHUNK)
            g = jnp.dot(n, wg_ref[:, sl], preferred_element_type=F32)
            u = jnp.dot(n, wu_ref[:, sl], preferred_element_type=F32)
            a = (g * jax.nn.sigmoid(g) * u).astype(BF16)
            d = jnp.dot(a, wd_ref[sl, :], preferred_element_type=F32)
            if c == 0:
                acc_ref[rows] = d
            else:
                acc_ref[rows] += d
        h = x + 0.5 * acc_ref[rows]
        hn = h * _rms_scale(h) * g2_ref[...]
        if final:
            y_ref[rows] = hn
        else:
            h_ref[rows] = h
            hn_ref[rows] = hn.astype(BF16)


def _ffn(x, g1, wg, wu, wd, g2, *, final):
    t = x.shape[0]
    tm = min(FFN_TM, t)
    row = pl.BlockSpec((tm, D_MODEL), lambda i: (i, 0))
    if final:
        out_shape = jax.ShapeDtypeStruct((t, D_MODEL), F32)
        out_specs = row
    else:
        out_shape = (jax.ShapeDtypeStruct((t, D_MODEL), F32), jax.ShapeDtypeStruct((t, D_MODEL), BF16))
        out_specs = (row, row)
    return pl.pallas_call(
        functools.partial(_ffn_kernel, final=final),
        grid=(t // tm,),
        in_specs=[row, _const_spec((1, D_MODEL)), _const_spec((D_MODEL, D_FF)), _const_spec((D_MODEL, D_FF)),
                  _const_spec((D_FF, D_MODEL)), _const_spec((1, D_MODEL))],
        out_specs=out_specs,
        out_shape=out_shape,
        scratch_shapes=[pltpu.VMEM((tm, D_MODEL), F32)],
        compiler_params=pltpu.CompilerParams(dimension_semantics=("parallel",), vmem_limit_bytes=VMEM_LIMIT),
        name="ffn_final" if final else "ffn_mid",
    )(x, g1, wg, wu, wd, g2)


def _group_mean_matrix(n, group):
    idx = np.arange(n) // group
    return jnp.asarray((idx[:, None] == idx[None, :]).astype(np.float32) / group, dtype=BF16)


def _qkv_kernel(hn_ref, w_ref, wvt_ref, gq_ref, gk_ref, gm_ref, q_ref, k_ref, vt_ref):
    hn = hn_ref[...]
    gm = gm_ref[...]
    qd = N_Q_HEADS * HEAD_DIM

    def proj(c):
        return jnp.dot(hn, w_ref[:, c * KV_DIM:(c + 1) * KV_DIM], preferred_element_type=F32)

    def store_normed(c, x):
        is_q = c < qd // KV_DIM
        gain = gq_ref[:, c * KV_DIM:(c + 1) * KV_DIM] if is_q else gk_ref[...]
        ms = jnp.dot((x * x).astype(BF16), gm, preferred_element_type=F32)
        y = (x * lax.rsqrt(ms + EPS) * gain).astype(BF16)
        out_ref, head0 = (q_ref, c * N_KV_HEADS) if is_q else (k_ref, 0)
        for j in range(N_KV_HEADS):
            for rb in range(out_ref.shape[0]):
                out_ref[rb, head0 + j] = y[rb * ATTN_BLOCK:(rb + 1) * ATTN_BLOCK, j * HEAD_DIM:(j + 1) * HEAD_DIM]

    n_norm = (qd + KV_DIM) // KV_DIM
    x_next = proj(0)
    for c in range(n_norm):
        x = x_next
        if c + 1 < n_norm:
            x_next = proj(c + 1)
        else:
            vt = lax.dot_general(wvt_ref[...], hn, _NT, preferred_element_type=F32).astype(BF16)
            for g in range(N_KV_HEADS):
                for rb in range(vt_ref.shape[0]):
                    vt_ref[rb, g] = vt[g * HEAD_DIM:(g + 1) * HEAD_DIM, rb * ATTN_BLOCK:(rb + 1) * ATTN_BLOCK]
        store_normed(c, x)


def _attn_qkv(hn, w_in, w_vt, gq, gk):
    t = hn.shape[0]
    tm = min(QKV_TM, t)
    nrb = tm // ATTN_BLOCK
    in_dim = w_in.shape[1]
    blocks = lambda *dims: pl.BlockSpec((nrb,) + dims, lambda i: (i, 0, 0, 0))
    shape = lambda *dims: jax.ShapeDtypeStruct((t // ATTN_BLOCK,) + dims, BF16)
    return pl.pallas_call(
        _qkv_kernel,
        grid=(t // tm,),
        in_specs=[pl.BlockSpec((tm, D_MODEL), lambda i: (i, 0)), _const_spec((D_MODEL, in_dim)),
                  _const_spec((KV_DIM, D_MODEL)), _const_spec((1, D_MODEL)), _const_spec((1, KV_DIM)),
                  _const_spec((KV_DIM, KV_DIM))],
        out_specs=(blocks(N_Q_HEADS, ATTN_BLOCK, HEAD_DIM), blocks(N_KV_HEADS, ATTN_BLOCK, HEAD_DIM),
                   blocks(N_KV_HEADS, HEAD_DIM, ATTN_BLOCK)),
        out_shape=(shape(N_Q_HEADS, ATTN_BLOCK, HEAD_DIM), shape(N_KV_HEADS, ATTN_BLOCK, HEAD_DIM),
                   shape(N_KV_HEADS, HEAD_DIM, ATTN_BLOCK)),
        compiler_params=pltpu.CompilerParams(dimension_semantics=("parallel",), vmem_limit_bytes=VMEM_LIMIT),
        name="attn_qkv",
    )(hn, w_in, w_vt, gq, gk, _group_mean_matrix(KV_DIM, HEAD_DIM))


LOG2E = float(np.log2(np.e))
_ALIBI_SLOPES = [float(2.0 ** (-8.0 * (i + 1) / N_Q_HEADS)) for i in range(N_Q_HEADS)]
_MASKED = -1e30
_ONES_ROWS = 16


def _per_head_cols(values, cols_per_head, shape):
    col = lax.broadcasted_iota(jnp.int32, shape, 1)
    out = jnp.full(shape, values[-1], F32)
    for j in range(len(values) - 2, -1, -1):
        out = jnp.where(col < (j + 1) * cols_per_head, values[j], out)
    return out


def _attn_core_kernel(sink_ref, q_ref, kp_ref, kc_ref, kn_ref, vp_ref, vc_ref, vn_ref, h_ref, wo_ref,
                      out_ref, att_ref, bias_ref, *, n_steps):
    n = pl.program_id(1)
    blk = ATTN_BLOCK
    nq = ATTN_STEP_BLOCKS
    kb = 3 * blk
    gm = GQA_GROUP * blk

    @pl.when(n == 0)
    def _():
        key = lax.broadcasted_iota(jnp.int32, (kb, gm), 0)
        qry = lax.broadcasted_iota(jnp.int32, (kb, gm), 1) & (blk - 1)
        absd = jnp.abs(key - WINDOW - qry)
        absd_f = absd.astype(F32)
        for g in range(N_KV_HEADS):
            slopes = [_ALIBI_SLOPES[g * GQA_GROUP + j] * LOG2E for j in range(GQA_GROUP)]
            bias_ref[g] = jnp.where(absd <= WINDOW, -_per_head_cols(slopes, blk, (kb, gm)) * absd_f, _MASKED)

    def k_block(j, g):
        return kp_ref[0, g] if j < 0 else kn_ref[0, g] if j == nq else kc_ref[j, g]

    def vt_block(j, g):
        return vp_ref[0, g] if j < 0 else vn_ref[0, g] if j == nq else vc_ref[j, g]

    key1 = lax.broadcasted_iota(jnp.int32, (kb, 1), 0)
    ones = jnp.ones((_ONES_ROWS, kb), BF16)

    def scores(i, g):
        qg = jnp.concatenate([q_ref[i, g * GQA_GROUP + j] for j in range(GQA_GROUP)], axis=0)
        kg = jnp.concatenate([k_block(j, g) for j in (i - 1, i, i + 1)], axis=0)
        s = lax.dot_general(kg, qg, _NT, preferred_element_type=F32) + bias_ref[g]
        if i == 0:
            s = s + jnp.where((key1 < blk) & (n == 0), _MASKED, 0.0)
        if i == nq - 1:
            s = s + jnp.where((key1 >= 2 * blk) & (n == n_steps - 1), _MASKED, 0.0)
        return s

    def attend(i, g, s):
        sink = _per_head_cols([sink_ref[g * GQA_GROUP + j] * LOG2E for j in range(GQA_GROUP)], blk, (1, gm))
        m = jnp.maximum(jnp.max(s, axis=0, keepdims=True), sink)
        p = jnp.exp2(s - m).astype(BF16)
        vt = jnp.concatenate([vt_block(j, g) for j in (i - 1, i, i + 1)], axis=1)
        o = jnp.dot(jnp.concatenate([vt, ones], axis=0), p, preferred_element_type=F32)
        a = (o[:HEAD_DIM] / (o[HEAD_DIM:HEAD_DIM + 1] + jnp.exp2(sink - m))).astype(BF16)
        for j in range(GQA_GROUP):
            hd = g * GQA_GROUP + j
            att_ref[hd * HEAD_DIM:(hd + 1) * HEAD_DIM, i * blk:(i + 1) * blk] = a[:, j * blk:(j + 1) * blk]

    pending = []
    for i in range(nq):
        for g in range(N_KV_HEADS):
            pending.append((i, g, scores(i, g)))
            if len(pending) > ATTN_LOOKAHEAD:
                attend(*pending.pop(0))
    for item in pending:
        attend(*item)
    out_ref[0] = h_ref[0] + lax.dot_general(att_ref[...], wo_ref[...], _TN, preferred_element_type=F32)


def _attn_core(q, k, vt, sink, h, w_out):
    b, s, _ = h.shape
    nq = ATTN_STEP_BLOCKS
    nb = s // ATTN_BLOCK
    ns = nb // nq
    prev_blk = lambda n: jnp.maximum(n * nq - 1, 0)
    next_blk = lambda n: jnp.minimum(n * nq + nq, nb - 1)
    blocks = lambda dims, rows_of: pl.BlockSpec((rows_of[0],) + dims, lambda bi, n: (rows_of[1](bi, n), 0, 0, 0))
    mid = (nq, lambda bi, n: bi * ns + n)
    before = (1, lambda bi, n: bi * nb + prev_blk(n))
    after = (1, lambda bi, n: bi * nb + next_blk(n))
    q_dims = (N_Q_HEADS, ATTN_BLOCK, HEAD_DIM)
    k_dims = (N_KV_HEADS, ATTN_BLOCK, HEAD_DIM)
    v_dims = (N_KV_HEADS, HEAD_DIM, ATTN_BLOCK)
    step_rows = pl.BlockSpec((1, nq * ATTN_BLOCK, D_MODEL), lambda bi, n: (bi, n, 0))
    return pl.pallas_call(
        functools.partial(_attn_core_kernel, n_steps=ns),
        grid=(b, ns),
        in_specs=[pl.BlockSpec(memory_space=pltpu.SMEM),
                  blocks(q_dims, mid),
                  blocks(k_dims, before), blocks(k_dims, mid), blocks(k_dims, after),
                  blocks(v_dims, before), blocks(v_dims, mid), blocks(v_dims, after),
                  step_rows, _const_spec((D_MODEL, D_MODEL))],
        out_specs=step_rows,
        out_shape=jax.ShapeDtypeStruct((b, s, D_MODEL), F32),
        scratch_shapes=[pltpu.VMEM((D_MODEL, nq * ATTN_BLOCK), BF16),
                        pltpu.VMEM((N_KV_HEADS, 3 * ATTN_BLOCK, GQA_GROUP * ATTN_BLOCK), F32)],
        compiler_params=pltpu.CompilerParams(dimension_semantics=("arbitrary", "arbitrary"),
                                             vmem_limit_bytes=VMEM_LIMIT),
        name="attn_core",
    )(sink, q, k, k, k, vt, vt, vt, h, w_out)


def _forget_gate(z2, lb):
    log_lb = jnp.log2(lb)
    log_1m_lb = jnp.log1p(-lb) * LOG2E
    e = jnp.exp2(-jnp.abs(z2))
    ope = 1.0 + e
    cl = log_1m_lb + (jnp.minimum(z2, 0.0) - jnp.log2(ope))
    log_f = jnp.maximum(log_lb, cl) + jnp.log2(1.0 + jnp.exp2(-jnp.abs(log_lb - cl)))
    return log_f, (1.0 - lb) * (jnp.where(z2 >= 0.0, e, 1.0) / ope)


def _rec_proj_kernel(hn_ref, w_ref, lb_ref, q_ref, lf_fw_ref, lf_bw_ref, k_fw_ref, k_bw_ref, i_ref, g_ref, *z_refs):
    hn = hn_ref[...]
    f = D_MODEL
    cw = REC_PROJ_CHUNK
    n_chunks = f // cw

    def proj(col):
        return jnp.dot(hn, w_ref[:, col:col + cw], preferred_element_type=F32)

    def parked(c, d):
        return z_refs[c % 2].at[lax.rem(pl.program_id(0) + c // 2, 2), d]

    def project(c):
        sl = slice(c * cw, (c + 1) * cw)
        q_ref[:, sl] = proj(c * cw)
        parked(c, 0)[...] = proj(f + c * cw)
        parked(c, 1)[...] = proj(2 * f + c * cw)
        i_ref[:, sl] = proj(3 * f + c * cw).astype(BF16)
        g_ref[:, sl] = proj(4 * f + c * cw).astype(BF16)

    def gates(c):
        sl = slice(c * cw, (c + 1) * cw)
        lf_fw_ref[:, sl], k_fw_ref[:, sl] = _forget_gate(parked(c, 0)[...] * LOG2E, lb_ref[0:1, sl])
        lf_bw_ref[:, sl], k_bw_ref[:, sl] = _forget_gate(parked(c, 1)[...] * LOG2E, lb_ref[1:2, sl])

    project(0)
    for c in range(n_chunks):
        if c + 1 < n_chunks:
            project(c + 1)
        gates(c)


def _rec_proj(hn, w_in, lb):
    t = hn.shape[0]
    tm = min(REC_PROJ_TM, t)
    row = pl.BlockSpec((tm, D_MODEL), lambda i: (i, 0))
    f32o = jax.ShapeDtypeStruct((t, D_MODEL), F32)
    bf16o = jax.ShapeDtypeStruct((t, D_MODEL), BF16)
    return pl.pallas_call(
        _rec_proj_kernel,
        grid=(t // tm,),
        in_specs=[row, _const_spec((D_MODEL, w_in.shape[1])), _const_spec((2, D_MODEL))],
        out_specs=(row,) * 7,
        out_shape=(f32o, f32o, f32o, f32o, f32o, bf16o, bf16o),
        scratch_shapes=[pltpu.VMEM((2, 2, tm, REC_PROJ_CHUNK), F32)] * 2,
        compiler_params=pltpu.CompilerParams(dimension_semantics=("parallel",), vmem_limit_bytes=VMEM_LIMIT),
        name="rec_proj",
    )(hn, w_in, lb)


def _neg_abs(x):
    bits = lax.bitcast_convert_type(x, jnp.uint32) | jnp.uint32(0x80000000)
    return lax.bitcast_convert_type(bits, F32)


def _scan_intra(q, k, lf, *, reverse):
    c, dk = q.shape
    sub = 8
    nv = c // sub
    r = lax.broadcasted_iota(jnp.int32, (nv, sub, dk), 1)
    pos = (sub - 1 - r) if reverse else r

    def later(x, s):
        return pltpu.roll(x, (sub - s) if reverse else s, 1)

    def bit(m):
        return (pos & m) != 0

    p = lf.reshape(nv, sub, dk)
    for s in (1, 2, 4):
        p = p + jnp.where(pos >= s, later(p, s), 0.0)
    g1 = jnp.where(bit(1), later(p, 1), p)
    h1 = jnp.where(bit(1), p, later(p, sub - 1))
    g2 = jnp.where(bit(2), later(h1, 2), h1)
    h2 = jnp.where(bit(2), h1, later(h1, sub - 2))
    h2_shift = later(h2, 4)
    g4 = jnp.where(bit(4), h2_shift, h2)
    tot8 = jnp.where(bit(4), h2, h2_shift)
    order = range(nv - 1, -1, -1) if reverse else range(nv)
    run = [None] * nv
    b = [None] * nv
    prev = None
    for a in order:
        b[a] = p[a] if prev is None else p[a] + prev
        run[a] = tot8[a] if prev is None else tot8[a] + prev
        prev = run[a]
    tot = prev
    b = jnp.stack(b)
    run = jnp.stack(run)
    q3 = q.reshape(nv, sub, dk)
    k3 = k.reshape(nv, sub, dk)

    rowi = lax.broadcasted_iota(jnp.int32, (c, c), 0)
    coli = lax.broadcasted_iota(jnp.int32, (c, c), 1)
    att = jnp.where(rowi == coli, jnp.sum(q * k, axis=-1, keepdims=True), 0.0)

    def add_level(att, m, x):
        gram = lax.dot_general(x, x, _NT, preferred_element_type=F32)
        r_odd = (rowi & m) != 0
        c_odd = (coli & m) != 0
        same = (rowi & ~(2 * m - 1)) == (coli & ~(2 * m - 1))
        pair = ((~r_odd & c_odd) if reverse else (r_odd & ~c_odd)) & same
        return jnp.where(pair, gram, att)

    for m, g in ((1, g1), (2, g2), (4, g4)):
        z = _neg_abs(p - g)
        x = (jnp.where(bit(m), q3, k3) * jnp.exp2(z)).astype(BF16).reshape(c, dk)
        att = add_level(att, m, x)
    mb = 1
    while mb < nv:
        shp = (nv // (2 * mb), 2, mb, sub, dk)
        first, second = (1, 0) if reverse else (0, 1)
        b5, q5, k5, run5 = (t.reshape(shp) for t in (b, q3, k3, run))
        g = run5[:, first, (0 if reverse else mb - 1)][:, None]
        x_first = k5[:, first] * jnp.exp2(g - b5[:, first])
        x_second = q5[:, second] * jnp.exp2(b5[:, second] - g)
        halves = [x_second, x_first] if reverse else [x_first, x_second]
        x = jnp.stack(halves, axis=1).astype(BF16).reshape(c, dk)
        att = add_level(att, sub * mb, x)
        mb *= 2
    b2 = b.reshape(c, dk)
    q_in = (q * jnp.exp2(b2)).astype(BF16)
    k_out = (k * jnp.exp2(tot[0:1, :] - b2)).astype(BF16)
    return att.astype(BF16), q_in, k_out, tot[0:1, :]


def _scan_inter(att, q_in, k_out, tot, v, st):
    o = lax.dot_general(q_in, st.astype(BF16), _NT, preferred_element_type=F32)
    o = o + jnp.dot(att, v, preferred_element_type=F32)
    st_new = st * jnp.exp2(tot) + lax.dot_general(v, k_out, _TN, preferred_element_type=F32)
    return o, st_new


def _rec_scan_kernel(*refs, reverse):
    if reverse:
        (q_ref, k_ref, lf_ref, v_ref, of_ref, g_ref, gain_ref, h_ref, wo_ref, out_ref, st_ref, att_ref, qin_ref,
         kout_ref, tot_ref, y_ref) = refs
    else:
        q_ref, k_ref, lf_ref, v_ref, of_ref, st_ref, att_ref, qin_ref, kout_ref, tot_ref = refs

    @pl.when(pl.program_id(1) == 0)
    def _():
        st_ref[...] = jnp.zeros_like(st_ref)

    n_chunks = SCAN_BLOCK // SCAN_CHUNK
    chunk_order = list(reversed(range(n_chunks)) if reverse else range(n_chunks))
    def intra(cc, hd):
        rows = slice(cc * SCAN_CHUNK, (cc + 1) * SCAN_CHUNK)
        cols = slice(hd * REC_EXPAND, (hd + 1) * REC_EXPAND)
        att_ref[cc, hd], qin_ref[rows, cols], kout_ref[rows, cols], tot_ref[cc, :, cols] = _scan_intra(
            q_ref[0, rows, cols], k_ref[0, rows, cols], lf_ref[0, rows, cols], reverse=reverse)

    def inter(cc, hd):
        rows = slice(cc * SCAN_CHUNK, (cc + 1) * SCAN_CHUNK)
        cols = slice(hd * REC_EXPAND, (hd + 1) * REC_EXPAND)
        o, st_new = _scan_inter(att_ref[cc, hd], qin_ref[rows, cols], kout_ref[rows, cols],
                                tot_ref[cc, :, cols], v_ref[0, rows, cols], st_ref[hd])
        st_ref[hd] = st_new
        if reverse:
            o = o + of_ref[0, rows, cols].astype(F32)
            g = g_ref[0, rows, cols].astype(F32)
            y = o * _rms_scale(o) * gain_ref[...] * (g * jax.nn.sigmoid(g))
            y_ref[rows, cols] = y.astype(BF16)
        else:
            of_ref[0, rows, cols] = o.astype(BF16)

    for idx, cc in enumerate(chunk_order):
        for hd in range(REC_HEADS):
            intra(cc, hd)
            if idx > 0:
                inter(chunk_order[idx - 1], hd)
    for hd in range(REC_HEADS):
        inter(chunk_order[-1], hd)
    if reverse:
        out_ref[0] = h_ref[0] + jnp.dot(y_ref[...], wo_ref[...], preferred_element_type=F32)


def _rec_scan(q, k, lf, v, tail=None, *, reverse):
    b, s, _ = q.shape
    nblk = s // SCAN_BLOCK
    blk_of = (lambda j: nblk - 1 - j) if reverse else (lambda j: j)
    blk = pl.BlockSpec((1, SCAN_BLOCK, D_MODEL), lambda bi, j: (bi, blk_of(j), 0))
    in_specs = [blk, blk, blk, blk]
    n_chunks = SCAN_BLOCK // SCAN_CHUNK
    scratch = [pltpu.VMEM((REC_HEADS, REC_V_DIM, REC_EXPAND), F32),
               pltpu.VMEM((n_chunks, REC_HEADS, SCAN_CHUNK, SCAN_CHUNK), BF16),
               pltpu.VMEM((SCAN_BLOCK, D_MODEL), BF16), pltpu.VMEM((SCAN_BLOCK, D_MODEL), BF16),
               pltpu.VMEM((n_chunks, 1, D_MODEL), F32)]
    args = (q, k, lf, v)
    if reverse:
        in_specs += [blk, blk, _const_spec((1, REC_V_DIM)), blk, _const_spec((D_MODEL, D_MODEL))]
        scratch += [pltpu.VMEM((SCAN_BLOCK, D_MODEL), BF16)]
        args += tuple(tail)
    return pl.pallas_call(
        functools.partial(_rec_scan_kernel, reverse=reverse),
        grid=(b, nblk),
        in_specs=in_specs,
        out_specs=blk,
        out_shape=jax.ShapeDtypeStruct((b, s, D_MODEL), F32 if reverse else BF16),
        scratch_shapes=scratch,
        compiler_params=pltpu.CompilerParams(dimension_semantics=("parallel", "arbitrary"),
                                             vmem_limit_bytes=VMEM_LIMIT),
        name="rec_scan_bwd_out" if reverse else "rec_scan_fwd",
    )(*args)


def _trunk(x, p):
    b, s, d = x.shape
    t = b * s
    x = x.reshape(t, d)
    for layer in range(DEPTH):
        j = layer // N_MIXERS
        h, hn = _ffn(x, p["ffn1_norm"][layer], p["ffn1_w_gate"][layer], p["ffn1_w_up"][layer],
                     p["ffn1_w_down"][layer], p["mix_norm"][layer], final=False)
        if layer % N_MIXERS == 0:
            q, k, vt = _attn_qkv(hn, p["attn_w_in"][j], p["attn_w_vt"][j], p["attn_gq"][j], p["attn_gk"][j])
            h = _attn_core(q, k, vt, p["attn_sink"][j], h.reshape(b, s, d), p["attn_w_out"][j]).reshape(t, d)
        else:
            q, lf_fw, lf_bw, k_fw, k_bw, iv, g = _rec_proj(hn, p["rec_w_in"][j], p["lb"][j])
            r3 = lambda a: a.reshape(b, s, d)
            o_fw = _rec_scan(r3(q), r3(k_fw), r3(lf_fw), r3(iv), reverse=False)
            tail = (o_fw, r3(g), p["rec_o_gain"][j], r3(h), p["rec_w_out"][j])
            h = _rec_scan(r3(q), r3(k_bw), r3(lf_bw), r3(iv), tail, reverse=True).reshape(t, d)
        x = _ffn(h, p["ffn2_norm"][layer], p["ffn2_w_gate"][layer], p["ffn2_w_up"][layer],
                 p["ffn2_w_down"][layer], p["out_norm"][layer], final=True)
    return x.reshape(b, s, d)


def kernel(x_prompt, x_sample, ffn1_norm, ffn1_w_gate, ffn1_w_up, ffn1_w_down, mix_norm, attn_w_in, attn_q_gain,
           attn_k_gain, attn_sink, attn_w_out, rec_w_in, rec_lb_logits, rec_o_gain, rec_w_out, ffn2_norm,
           ffn2_w_gate, ffn2_w_up, ffn2_w_down, out_norm):
    lb = jnp.cumsum(jax.nn.softmax(rec_lb_logits.astype(F32), axis=0), axis=0)
    lb = lb - lb[0:1]
    row = lambda a: a.astype(F32)[:, None, :]
    p = {
        "ffn1_norm": row(ffn1_norm), "mix_norm": row(mix_norm), "ffn2_norm": row(ffn2_norm),
        "out_norm": row(out_norm),
        "ffn1_w_gate": ffn1_w_gate.astype(BF16), "ffn1_w_up": ffn1_w_up.astype(BF16),
        "ffn1_w_down": ffn1_w_down.astype(BF16),
        "ffn2_w_gate": ffn2_w_gate.astype(BF16), "ffn2_w_up": ffn2_w_up.astype(BF16),
        "ffn2_w_down": ffn2_w_down.astype(BF16),
        "attn_w_in": attn_w_in.astype(BF16), "attn_w_out": attn_w_out.astype(BF16),
        "attn_w_vt": jnp.swapaxes(attn_w_in[:, :, D_MODEL + KV_DIM:], 1, 2).astype(BF16),
        "attn_gq": jnp.tile(attn_q_gain.astype(F32), (1, N_Q_HEADS))[:, None, :] * (HEAD_DIM ** -0.5 * LOG2E),
        "attn_gk": jnp.tile(attn_k_gain.astype(F32), (1, N_KV_HEADS))[:, None, :],
        "attn_sink": attn_sink.astype(F32),
        "rec_w_in": rec_w_in.astype(BF16), "rec_w_out": rec_w_out.astype(BF16),
        "rec_o_gain": row(rec_o_gain), "lb": lb,
    }
    return (_trunk(x_prompt, p), _trunk(x_sample, p))
```
